```python
import jax, jax.numpy as jnp
from jax import lax
import numpy as np

D_MODEL = 1024
BATCH = 32
SEQ = 256
DEPTH = 1
DEC_BATCH = 2
DEC_SEQ = 4096
PAST_LEN = 512

GRID_W = 64
HEAD_DIM = 64
N_HEADS = D_MODEL // HEAD_DIM
D_WKV = N_HEADS * HEAD_DIM
DECAY_RANK = 64
ICLR_RANK = 64
GATE_RANK = 128
D_RNN = 3 * D_MODEL // 2
N_LRU_BLOCKS = 16
LRU_BLOCK = D_RNN // N_LRU_BLOCKS
CONV_W = 4
LRU_C = 8.0
N_EXPERTS = 32
TOP_K = 4
D_EXPERT = D_MODEL
SWIGLU_LIMIT = 7.0
SWIGLU_ALPHA = 1.702
RMS_EPS = 1e-6
GN_EPS = 64e-5
N_BRANCH = 2
N_SHIFT = 3 * D_WKV + DECAY_RANK + ICLR_RANK + GATE_RANK
SHIFT_SPLITS = (D_WKV, 2 * D_WKV, 3 * D_WKV, 3 * D_WKV + DECAY_RANK, 3 * D_WKV + DECAY_RANK + ICLR_RANK)
N_IN = N_SHIFT + 2 * D_RNN + N_BRANCH * D_MODEL

kernel_name = 'hybrid_rwkv7_rglru_moe_diffusion_step'


def rms_norm(x, g):
    xf = x.astype(jnp.float32)
    y = xf * lax.rsqrt(jnp.mean(xf * xf, axis=-1, keepdims=True) + RMS_EPS)
    return (y * g.astype(jnp.float32)).astype(x.dtype)


def shift_seq(p):
    half = p.shape[-1] // 2
    prev = jnp.pad(p[:, :-1, :half], ((0, 0), (1, 0), (0, 0)))
    nxt = jnp.pad(p[:, 1:, half:], ((0, 0), (0, 1), (0, 0)))
    return jnp.concatenate([prev, nxt], axis=-1)


def shift_grid(p):
    b, t, ch = p.shape
    rows = t // GRID_W
    g = p.reshape(b, rows, GRID_W, ch)
    q = ch // 4
    left = jnp.pad(g[:, :, :-1, :q], ((0, 0), (0, 0), (1, 0), (0, 0)))
    right = jnp.pad(g[:, :, 1:, q:2 * q], ((0, 0), (0, 0), (0, 1), (0, 0)))
    up = jnp.pad(g[:, :-1, :, 2 * q:3 * q], ((0, 0), (1, 0), (0, 0), (0, 0)))
    down = jnp.pad(g[:, 1:, :, 3 * q:], ((0, 0), (0, 1), (0, 0), (0, 0)))
    return jnp.concatenate([left, right, up, down], axis=-1).reshape(b, t, ch)


def dwconv(x, w, b):
    t = x.shape[1]
    left = CONV_W // 2
    xp = jnp.pad(x, ((0, 0), (left, CONV_W - 1 - left), (0, 0)))
    y = b
    for j in range(CONV_W):
        y = y + xp[:, j:j + t] * w[j]
    return y


def wkv_scan(r, w, k, v, kk, bb, s0, reverse):
    def step(s, inp):
        r_t, w_t, k_t, v_t, kk_t, b_t = inp
        sa = jnp.einsum('bhvk,bhk->bhv', s, -kk_t)
        s = s * w_t[:, :, None, :] + sa[..., None] * b_t[:, :, None, :] + v_t[..., None] * k_t[:, :, None, :]
        return s, jnp.einsum('bhvk,bhk->bhv', s, r_t)
    xs = tuple(jnp.swapaxes(z, 0, 1) for z in (r, w, k, v, kk, bb))
    s_fin, o = lax.scan(step, s0, xs, reverse=reverse)
    return jnp.swapaxes(o, 0, 1), s_fin


def linear_scan(a, b, h0, reverse):
    edge = -1 if reverse else 0
    b = b.at[:, edge].add(a[:, edge] * h0)

    def comb(l, r):
        al, bl = l
        ar, br = r
        return ar * al, ar * bl + br
    _, h = lax.associative_scan(comb, (a, b), reverse=reverse, axis=1)
    return h


def rwkv7_branch(r, k, v, wd, ad, gd, s0, lp):
    bsz, t, _ = r.shape
    f32 = jnp.float32

    def heads(z):
        return z.astype(f32).reshape(bsz, t, N_HEADS, HEAD_DIM)
    g = jax.nn.sigmoid(gd) @ lp['wkv_g_up']
    kk = heads(k * lp['wkv_k_k'])
    kk = kk * lax.rsqrt(jnp.maximum(jnp.sum(kk * kk, axis=-1, keepdims=True), 1e-24))
    rh = heads(r)
    vh = heads(v)
    wd_t = jnp.tanh(wd.astype(f32))
    outs, bonus, finals = [], [], []
    for d in range(2):
        w_log = -jax.nn.softplus(-(lp['wkv_w0'][d].astype(f32) + wd_t @ lp['wkv_w_up'][d].astype(f32))) - 0.5
        decay = jnp.exp(-jnp.exp(w_log))
        a = jax.nn.sigmoid(lp['wkv_a0'][d].astype(f32) + ad.astype(f32) @ lp['wkv_a_up'][d].astype(f32))
        kd = heads(k.astype(f32) * (1.0 + (a - 1.0) * lp['wkv_k_a'].astype(f32)))
        o, s_fin = wkv_scan(rh, heads(decay), kd, vh, kk, kk * heads(a), s0[:, d].astype(f32), d == 1)
        outs.append(o)
        finals.append(s_fin)
        bonus.append(jnp.sum(rh * kd * lp['wkv_r_k'][d].astype(f32), axis=-1, keepdims=True) * vh)
    o = outs[0] + outs[1]
    mu = jnp.mean(o, axis=-1, keepdims=True)
    var = jnp.mean(jnp.square(o - mu), axis=-1, keepdims=True)
    on = ((o - mu) * lax.rsqrt(var + GN_EPS)).reshape(bsz, t, D_WKV)
    y = (on * lp['wkv_ln_g'].astype(f32) + lp['wkv_ln_b'].astype(f32)
         + (bonus[0] + bonus[1]).reshape(bsz, t, D_WKV)) * g.astype(f32)
    return y.astype(r.dtype), jnp.stack(finals, axis=1)


def rglru_branch(lx, lg, h0, lp):
    bsz, t, _ = lx.shape
    f32 = jnp.float32
    xb = dwconv(lx, lp['lru_conv_w'], lp['lru_conv_b'])
    xblk = xb.reshape(bsz, t, N_LRU_BLOCKS, LRU_BLOCK)
    hs, finals = [], []
    for d in range(2):
        r_g = jax.nn.sigmoid(jnp.einsum('btni,nij->btnj', xblk, lp['lru_wa'][d]).reshape(bsz, t, D_RNN)
                             + lp['lru_ba'][d]).astype(f32)
        i_g = jax.nn.sigmoid(jnp.einsum('btni,nij->btnj', xblk, lp['lru_wx'][d]).reshape(bsz, t, D_RNN)
                             + lp['lru_bx'][d]).astype(f32)
        log_a = LRU_C * r_g * jax.nn.log_sigmoid(lp['lru_lambda'][d].astype(f32))
        a = jnp.exp(log_a)
        u = jnp.sqrt(-jnp.expm1(2.0 * log_a)) * i_g * xb.astype(f32)
        h = linear_scan(a, u, h0[:, d].astype(f32), d == 1)
        hs.append(h)
        finals.append(h[:, 0] if d == 1 else h[:, -1])
    y = ((hs[0] + hs[1]).astype(lx.dtype) * jax.nn.gelu(lg)) @ lp['lru_proj']
    return y, jnp.stack(finals, axis=1)


def moe(h, lp):
    b, t, d = h.shape
    hf = h.reshape(b * t, d)
    logits = (hf @ lp['router_w'] + lp['router_b']).astype(jnp.float32)
    top_val, top_idx = lax.top_k(logits, TOP_K)
    top_w = jax.nn.softmax(top_val, axis=-1)
    combine = jnp.einsum('nk,nke->ne', top_w, jax.nn.one_hot(top_idx, N_EXPERTS, dtype=jnp.float32))
    out = jnp.zeros((b * t, d), jnp.float32)
    for e in range(N_EXPERTS):
        gu = hf @ lp['exp_w_gu'][e] + lp['exp_b_gu'][e]
        gate = jnp.minimum(gu[:, :D_EXPERT], SWIGLU_LIMIT)
        up = jnp.clip(gu[:, D_EXPERT:], -SWIGLU_LIMIT, SWIGLU_LIMIT)
        act = gate * jax.nn.sigmoid(SWIGLU_ALPHA * gate) * (up + 1.0)
        out = out + combine[:, e:e + 1] * (act @ lp['exp_w_down'][e] + lp['exp_b_down'][e]).astype(jnp.float32)
    return out.astype(h.dtype).reshape(b, t, d)


def trunk_layer(x, cond, s_wkv0, s_lru0, shift_fn, lp):
    mod = (jax.nn.silu(cond) @ lp['w_ada'] + lp['b_ada'])[:, None, :]
    sh1, sc1, gt1, sh2, sc2, gt2 = jnp.split(mod, 6, axis=-1)
    h = rms_norm(x, lp['norm_mix_g']) * (1.0 + sc1) + sh1
    p = h @ lp['w_in']
    ps = p[..., :N_SHIFT]
    shifted = jnp.concatenate([shift_fn(z) for z in jnp.split(ps, SHIFT_SPLITS, axis=-1)], axis=-1)
    ps = ps + lp['shift_mu'] * (shifted - ps)
    r, k, v, wd, ad, gd = jnp.split(ps, SHIFT_SPLITS, axis=-1)
    lx = p[..., N_SHIFT:N_SHIFT + D_RNN]
    lg = p[..., N_SHIFT + D_RNN:N_SHIFT + 2 * D_RNN]
    gate_a, gate_b = jnp.split(jax.nn.sigmoid(p[..., N_SHIFT + 2 * D_RNN:] + lp['b_merge']), 2, axis=-1)
    y_a, st_wkv = rwkv7_branch(r, k, v, wd, ad, gd, s_wkv0, lp)
    y_b, st_lru = rglru_branch(lx, lg, s_lru0, lp)
    x = x + gt1 * ((gate_a * y_a + gate_b * y_b) @ lp['w_out'])
    h2 = rms_norm(x, lp['norm_ffn_g']) * (1.0 + sc2) + sh2
    x = x + gt2 * moe(h2, lp)
    return x, st_wkv, st_lru


def setup_inputs(seed: int = 0) -> dict:
    key = jax.random.key(seed)
    ks = list(jax.random.split(key, 48))
    f32 = jnp.float32
    L = DEPTH
    s_in = D_MODEL ** -0.5

    def nrm(i, shape, scale):
        return jax.random.normal(ks[i], shape, f32) * scale
    t_a = jax.random.uniform(ks[40], (L, 2, D_RNN), f32, minval=0.9, maxval=0.999)
    s_a = t_a ** (1.0 / LRU_C)
    return {
        'x_prompt': nrm(0, (BATCH, SEQ, D_MODEL), 1.0),
        'x_sample': nrm(1, (DEC_BATCH, DEC_SEQ, D_MODEL), 1.0),
        'state_wkv': nrm(2, (DEC_BATCH, L, 2, N_HEADS, HEAD_DIM, HEAD_DIM), 0.3),
        'state_lru': nrm(3, (DEC_BATCH, L, 2, D_RNN), 0.5),
        'c': nrm(4, (DEC_BATCH, D_MODEL), 1.0),
        'c_ctx': nrm(5, (D_MODEL,), 1.0),
        'norm_mix_g': 1.0 + nrm(6, (L, D_MODEL), 0.05),
        'norm_ffn_g': 1.0 + nrm(7, (L, D_MODEL), 0.05),
        'w_ada': nrm(8, (L, D_MODEL, 6 * D_MODEL), 0.5 * s_in),
        'b_ada': nrm(9, (L, 6 * D_MODEL), 0.02),
        'w_in': nrm(10, (L, D_MODEL, N_IN), s_in),
        'shift_mu': jax.random.uniform(ks[11], (L, N_SHIFT), f32),
        'wkv_k_k': 0.85 + nrm(12, (L, D_WKV), 0.05),
        'wkv_k_a': 1.0 + nrm(13, (L, D_WKV), 0.05),
        'wkv_r_k': nrm(14, (L, 2, N_HEADS, HEAD_DIM), 0.1),
        'wkv_w0': jax.random.uniform(ks[15], (L, 2, D_WKV), f32, minval=-5.0, maxval=-0.5),
        'wkv_w_up': nrm(16, (L, 2, DECAY_RANK, D_WKV), DECAY_RANK ** -0.5),
        'wkv_a0': nrm(17, (L, 2, D_WKV), 0.5),
        'wkv_a_up': nrm(18, (L, 2, ICLR_RANK, D_WKV), ICLR_RANK ** -0.5),
        'wkv_g_up': nrm(19, (L, GATE_RANK, D_WKV), GATE_RANK ** -0.5),
        'wkv_ln_g': 1.0 + nrm(20, (L, D_WKV), 0.05),
        'wkv_ln_b': nrm(21, (L, D_WKV), 0.02),
        'lru_conv_w': nrm(22, (L, CONV_W, D_RNN), CONV_W ** -0.5),
        'lru_conv_b': nrm(23, (L, D_RNN), 0.02),
        'lru_wa': nrm(24, (L, 2, N_LRU_BLOCKS, LRU_BLOCK, LRU_BLOCK), LRU_BLOCK ** -0.5),
        'lru_ba': nrm(25, (L, 2, D_RNN), 0.02),
        'lru_wx': nrm(26, (L, 2, N_LRU_BLOCKS, LRU_BLOCK, LRU_BLOCK), LRU_BLOCK ** -0.5),
        'lru_bx': nrm(27, (L, 2, D_RNN), 0.02),
        'lru_lambda': jnp.log(s_a) - jnp.log1p(-s_a),
        'lru_proj': nrm(28, (L, D_RNN, D_MODEL), D_RNN ** -0.5),
        'b_merge': nrm(29, (L, N_BRANCH * D_MODEL), 0.02),
        'w_out': nrm(30, (L, D_MODEL, D_MODEL), s_in),
        'router_w': nrm(31, (L, D_MODEL, N_EXPERTS), s_in),
        'router_b': nrm(32, (L, N_EXPERTS), 0.01),
        'exp_w_gu': nrm(33, (L, N_EXPERTS, D_MODEL, 2 * D_EXPERT), s_in),
        'exp_b_gu': nrm(34, (L, N_EXPERTS, 2 * D_EXPERT), 0.01),
        'exp_w_down': nrm(35, (L, N_EXPERTS, D_EXPERT, D_MODEL), D_EXPERT ** -0.5),
        'exp_b_down': nrm(36, (L, N_EXPERTS, D_MODEL), 0.01),
        'final_norm_g': 1.0 + nrm(37, (D_MODEL,), 0.05),
    }


def reference(x_prompt, x_sample, state_wkv, state_lru, c, c_ctx, norm_mix_g, norm_ffn_g, w_ada, b_ada,
              w_in, shift_mu, wkv_k_k, wkv_k_a, wkv_r_k, wkv_w0, wkv_w_up, wkv_a0, wkv_a_up, wkv_g_up,
              wkv_ln_g, wkv_ln_b, lru_conv_w, lru_conv_b, lru_wa, lru_ba, lru_wx, lru_bx, lru_lambda,
              lru_proj, b_merge, w_out, router_w, router_b, exp_w_gu, exp_b_gu, exp_w_down, exp_b_down,
              final_norm_g):
    n_ctx = x_prompt.shape[0]
    cond_ctx = jnp.broadcast_to(c_ctx, (n_ctx, D_MODEL))
    zero_wkv = jnp.zeros((n_ctx, 2, N_HEADS, HEAD_DIM, HEAD_DIM), jnp.float32)
    zero_lru = jnp.zeros((n_ctx, 2, D_RNN), jnp.float32)
    xp = x_prompt
    xs = x_sample
    new_wkv, new_lru = [], []
    for l in range(DEPTH):
        lp = {
            'norm_mix_g': norm_mix_g[l], 'norm_ffn_g': norm_ffn_g[l], 'w_ada': w_ada[l], 'b_ada': b_ada[l],
            'w_in': w_in[l], 'shift_mu': shift_mu[l], 'wkv_k_k': wkv_k_k[l], 'wkv_k_a': wkv_k_a[l],
            'wkv_r_k': wkv_r_k[l], 'wkv_w0': wkv_w0[l], 'wkv_w_up': wkv_w_up[l], 'wkv_a0': wkv_a0[l],
            'wkv_a_up': wkv_a_up[l], 'wkv_g_up': wkv_g_up[l], 'wkv_ln_g': wkv_ln_g[l], 'wkv_ln_b': wkv_ln_b[l],
            'lru_conv_w': lru_conv_w[l], 'lru_conv_b': lru_conv_b[l], 'lru_wa': lru_wa[l], 'lru_ba': lru_ba[l],
            'lru_wx': lru_wx[l], 'lru_bx': lru_bx[l], 'lru_lambda': lru_lambda[l], 'lru_proj': lru_proj[l],
            'b_merge': b_merge[l], 'w_out': w_out[l], 'router_w': router_w[l], 'router_b': router_b[l],
            'exp_w_gu': exp_w_gu[l], 'exp_b_gu': exp_b_gu[l], 'exp_w_down': exp_w_down[l],
            'exp_b_down': exp_b_down[l],
        }
        xp, st_w, st_l = trunk_layer(xp, cond_ctx, zero_wkv, zero_lru, shift_seq, lp)
        new_wkv.append(st_w)
        new_lru.append(st_l)
        xs, _, _ = trunk_layer(xs, c, state_wkv[:, l], state_lru[:, l], shift_grid, lp)
    y_prompt = rms_norm(xp, final_norm_g)
    y_sample = rms_norm(xs, final_norm_g)
    new_state_wkv = jnp.stack(new_wkv, axis=1).astype(state_wkv.dtype)
    new_state_lru = jnp.stack(new_lru, axis=1).astype(state_lru.dtype)
    return (y_prompt, y_sample, new_state_wkv, new_state_lru)
```

```python
import functools

import numpy as np
import jax
import jax.numpy as jnp
from jax import lax
from jax.experimental import pallas as pl
from jax.experimental.pallas import tpu as pltpu

F32 = jnp.float32
BF16 = jnp.bfloat16

D_MODEL = 1024
HEAD_DIM = 64
N_HEADS = 16
GRID_W = 64
D_RNN = 1536
LRU_BLOCK = 96
CONV_W = 4
LRU_C = 8.0
TOP_K = 4
SWIGLU_LIMIT = 7.0
SWIGLU_ALPHA = 1.702
RMS_EPS = 1e-6
GN_EPS = 64e-5

COL_LX = 0
COL_LG = 1536
COL_R = 3072
COL_K = 4096
COL_V = 5120
COL_GA = 6144
COL_GB = 7168
COL_SMALL = 8192
N_IN = 8448

HG = 256
WKV_CHUNK = 64
VMEM_LIMIT = 56 * 1024 * 1024


def _cparams(sem, vmem=VMEM_LIMIT):
    return pltpu.CompilerParams(dimension_semantics=sem, vmem_limit_bytes=vmem)


def _split2(x):
    hi = x.astype(BF16)
    lo = (x - hi.astype(F32)).astype(BF16)
    return hi, lo


def _dot(a, b):
    return jnp.dot(a, b, preferred_element_type=F32)


def _dot3(a, b):
    ah, al = _split2(a)
    bh, bl = _split2(b)
    return _dot(ah, bh) + (_dot(al, bh) + _dot(ah, bl))


def _dot_exact_rhs(a, b_bf16):
    ah, al = _split2(a)
    return _dot(ah, b_bf16) + _dot(al, b_bf16)


def _head_ones(n):
    r = lax.broadcasted_iota(jnp.int32, (n, n), 0) // HEAD_DIM
    c = lax.broadcasted_iota(jnp.int32, (n, n), 1) // HEAD_DIM
    return jnp.where(r == c, 1.0, 0.0).astype(BF16)


def _sigmoid(x):
    return 1.0 / (1.0 + jnp.exp(-x))


def _softplus(x):
    return jnp.maximum(x, 0.0) + jnp.log(1.0 + jnp.exp(-jnp.abs(x)))


def _mod_kernel(c_ref, w_ref, b_ref, o_ref):
    c = c_ref[...]
    s = c * _sigmoid(c)
    o_ref[...] = _dot3(s, w_ref[...]) + b_ref[...]


def _modulation(cond8, w_ada, b_ada):
    n = w_ada.shape[1]
    tn = 1024
    return pl.pallas_call(
        _mod_kernel,
        grid=(n // tn,),
        in_specs=[
            pl.BlockSpec((8, D_MODEL), lambda j: (0, 0)),
            pl.BlockSpec((D_MODEL, tn), lambda j: (0, j)),
            pl.BlockSpec((1, tn), lambda j: (0, j)),
        ],
        out_specs=pl.BlockSpec((8, tn), lambda j: (0, j)),
        out_shape=jax.ShapeDtypeStruct((8, n), F32),
        compiler_params=_cparams(("parallel",)),
        name="modulation",
    )(cond8, w_ada, b_ada.reshape(1, n))


def _inproj_kernel(x_ref, mod_ref, g_ref, w_ref, o_ref, h_ref):
    @pl.when(pl.program_id(1) == 0)
    def _():
        x = x_ref[...]
        ms = jnp.mean(x * x, axis=-1, keepdims=True)
        y = x * lax.rsqrt(ms + RMS_EPS) * g_ref[...]
        m = mod_ref[...]
        sh1 = m[:, 0:D_MODEL]
        sc1 = m[:, D_MODEL:2 * D_MODEL]
        h_ref[...] = (y * (1.0 + sc1) + sh1).astype(BF16)

    o_ref[...] = _dot(h_ref[...], w_ref[...])


def _in_projection(x, mod3, mod_base, rows_per_mod, g, w_bf16):
    t = x.shape[0]
    tm, tn = 1024, 768
    n = w_bf16.shape[1]
    return pl.pallas_call(
        _inproj_kernel,
        grid=(t // tm, n // tn),
        in_specs=[
            pl.BlockSpec((tm, D_MODEL), lambda i, j: (i, 0)),
            pl.BlockSpec((None, 1, 6 * D_MODEL), lambda i, j: (mod_base + (i * tm) // rows_per_mod, 0, 0)),
            pl.BlockSpec((1, D_MODEL), lambda i, j: (0, 0)),
            pl.BlockSpec((D_MODEL, tn), lambda i, j: (0, j)),
        ],
        out_specs=pl.BlockSpec((tm, tn), lambda i, j: (i, j)),
        out_shape=jax.ShapeDtypeStruct((t, n), F32),
        scratch_shapes=[pltpu.VMEM((tm, D_MODEL), BF16)],
        compiler_params=_cparams(("parallel", "arbitrary")),
        name="in_projection",
    )(x, mod3, g.reshape(1, D_MODEL), w_bf16)


def _shift_seq_kernel(seq_len, p_ref, mu_ref, dir_ref, o_ref):
    x = p_ref[...]
    tr = x.shape[0]
    pos = lax.broadcasted_iota(jnp.int32, x.shape, 0) % seq_len
    prev = jnp.where(pos == 0, 0.0, pltpu.roll(x, 1, axis=0))
    nxt = jnp.where(pos == seq_len - 1, 0.0, pltpu.roll(x, tr - 1, axis=0))
    shifted = jnp.where(dir_ref[...] == 0, prev, nxt)
    o_ref[...] = x + mu_ref[...] * (shifted - x)


def _shift_grid_kernel(seq_len, p_ref, hp_ref, hn_ref, mu_ref, dir_ref, o_ref):
    x = p_ref[...]
    tr = x.shape[0]
    i = pl.program_id(0)
    tiles_per_seq = seq_len // tr
    first = (i % tiles_per_seq) == 0
    last = (i % tiles_per_seq) == tiles_per_seq - 1
    col = lax.broadcasted_iota(jnp.int32, x.shape, 0) % GRID_W
    left = jnp.where(col == 0, 0.0, pltpu.roll(x, 1, axis=0))
    right = jnp.where(col == GRID_W - 1, 0.0, pltpu.roll(x, tr - 1, axis=0))
    hp = jnp.where(first, 0.0, hp_ref[...])
    hn = jnp.where(last, 0.0, hn_ref[...])
    up = jnp.concatenate([hp, x[:tr - GRID_W]], axis=0)
    down = jnp.concatenate([x[GRID_W:], hn], axis=0)
    d = dir_ref[...]
    shifted = jnp.where(d == 0, left, jnp.where(d == 1, right, jnp.where(d == 2, up, down)))
    o_ref[...] = x + mu_ref[...] * (shifted - x)


def _dir_codes(widths, n_dirs):
    codes = []
    for w in widths:
        q = w // n_dirs
        codes.append(np.repeat(np.arange(n_dirs, dtype=np.int32), q))
    return np.concatenate(codes)[None, :]


def _token_shift(p, mu_perm, col0, width, group_widths, n_groups, seq_len, grid_mode):
    t = p.shape[0]
    tr = 512
    cb0 = col0 // width
    n_dirs = 4 if grid_mode else 2
    dirs = jnp.asarray(_dir_codes(group_widths, n_dirs))
    main = pl.BlockSpec((tr, width), lambda i, g: (i, cb0 + g))
    mu_spec = pl.BlockSpec((1, width), lambda i, g: (0, cb0 + g))
    dir_spec = pl.BlockSpec((1, width), lambda i, g: (0, 0))
    out_spec = pl.BlockSpec((None, tr, width), lambda i, g: (g, i, 0))
    out_shape = jax.ShapeDtypeStruct((n_groups, t, width), F32)
    if not grid_mode:
        return pl.pallas_call(
            functools.partial(_shift_seq_kernel, seq_len),
            grid=(t // tr, n_groups),
            in_specs=[main, mu_spec, dir_spec],
            out_specs=out_spec,
            out_shape=out_shape,
            compiler_params=_cparams(("parallel", "parallel")),
            name="token_shift_seq",
        )(p, mu_perm, dirs)
    hb = tr // GRID_W
    n_hb = t // GRID_W
    halo_prev = pl.BlockSpec((GRID_W, width), lambda i, g: (jnp.maximum(i * hb - 1, 0), cb0 + g))
    halo_next = pl.BlockSpec((GRID_W, width), lambda i, g: (jnp.minimum((i + 1) * hb, n_hb - 1), cb0 + g))
    return pl.pallas_call(
        functools.partial(_shift_grid_kernel, seq_len),
        grid=(t // tr, n_groups),
        in_specs=[main, halo_prev, halo_next, mu_spec, dir_spec],
        out_specs=out_spec,
        out_shape=out_shape,
        compiler_params=_cparams(("parallel", "parallel")),
        name="token_shift_grid",
    )(p, p, p, mu_perm, dirs)


def _bd_rows(x):
    c = x.shape[0]
    xt = jnp.concatenate([x, x, x, x], axis=0)
    rb = lax.broadcasted_iota(jnp.int32, xt.shape, 0) // c
    lb = lax.broadcasted_iota(jnp.int32, xt.shape, 1) // HEAD_DIM
    return jnp.where(rb == lb, xt, jnp.zeros_like(xt))


def _dot_nt(a, b):
    return lax.dot_general(a, b, (((1,), (1,)), ((), ())), preferred_element_type=F32)


def _dot_tn(a, b):
    return lax.dot_general(a, b, (((0,), (0,)), ((), ())), preferred_element_type=F32)


def _wkv_direction(rev, r, k, v, sm, kk_w, ka_w, w0, wup, a0, aup, s_ref, ones_bd):
    c = r.shape[0]
    wd = sm[:, 0:64]
    ad = sm[:, 64:128]
    kkr = k * kk_w
    ss = _dot_exact_rhs(kkr * kkr, ones_bd)
    kk = kkr * lax.rsqrt(jnp.maximum(ss, 1e-24))
    wlin = w0 + _dot3(jnp.tanh(wd), wup)
    w_log = -_softplus(-wlin) - 0.5
    logw = -jnp.exp(w_log)
    a = _sigmoid(a0 + _dot3(ad, aup))
    kd = k * (1.0 + (a - 1.0) * ka_w)
    b = kk * a

    ti = lax.broadcasted_iota(jnp.int32, (c, c), 0)
    si = lax.broadcasted_iota(jnp.int32, (c, c), 1)
    tri = jnp.where((si >= ti) if rev else (si <= ti), 1.0, 0.0).astype(BF16)
    lh = logw.astype(BF16)
    l1 = logw - lh.astype(F32)
    lm = l1.astype(BF16)
    ll = (l1 - lm.astype(F32)).astype(BF16)
    cum = _dot(tri, lh) + (_dot(tri, lm) + _dot(tri, ll))
    ltot = cum[0:1, :] if rev else cum[c - 1:c, :]
    e_inc = jnp.exp(cum)
    e_neg = jnp.exp(-cum)
    e_tail = jnp.exp(ltot - cum)
    at = -kk * jnp.exp(cum - logw)
    rt = r * e_inc
    bt = b * e_neg
    kt = kd * e_neg
    bh = b * e_tail
    kh = kd * e_tail

    ar = jnp.concatenate([at, rt], axis=0).astype(BF16)
    m_b = _dot_nt(ar, _bd_rows(bt).astype(BF16))
    m_k = _dot_nt(ar, _bd_rows(kt).astype(BF16))
    t2 = lax.broadcasted_iota(jnp.int32, (c, 4 * c), 0)
    s2 = lax.broadcasted_iota(jnp.int32, (c, 4 * c), 1) % c
    strict = (s2 > t2) if rev else (s2 < t2)
    incl = (s2 >= t2) if rev else (s2 <= t2)
    m_ab = jnp.where(strict, m_b[:c], 0.0)
    m_ak = jnp.where(strict, m_k[:c], 0.0)
    n_rb = jnp.where(incl, m_b[c:], 0.0)
    n_rk = jnp.where(incl, m_k[c:], 0.0)

    x = jnp.where(s2 == t2, 1.0, 0.0)
    blk = 1
    while blk < c:
        tb = t2 // blk
        sb = s2 // blk
        if rev:
            off = (tb % 2 == 0) & (sb == tb + 1)
        else:
            off = (tb % 2 == 1) & (sb == tb - 1)
        m_off = jnp.where(off, m_ab, 0.0)
        if blk == 1:
            x = x + m_off
        else:
            xb = x.astype(BF16)
            p1 = _dot(xb, _bd_rows(m_off).astype(BF16))
            x = x + _dot(p1.astype(BF16), _bd_rows(x).astype(BF16))
        blk *= 2

    s0 = s_ref[...]
    st = _dot_nt(ar, s0.astype(BF16))
    v_bd = _bd_rows(v).astype(BF16)
    rhs = st[:c] + _dot(m_ak.astype(BF16), v_bd)
    u = _dot(x.astype(BF16), _bd_rows(rhs).astype(BF16))
    u_bd = _bd_rows(u).astype(BF16)
    o = st[c:] + _dot(n_rb.astype(BF16), u_bd) + _dot(n_rk.astype(BF16), v_bd)
    uv = jnp.concatenate([u, v], axis=0).astype(BF16)
    bk = jnp.concatenate([bh, kh], axis=0).astype(BF16)
    upd = _dot_tn(uv, bk)
    s_ref[...] = s0 * jnp.exp(ltot) + jnp.where(ones_bd > 0, upd, 0.0)
    return o


def _wkv_kernel(zero_init, nc,
                rf_ref, kf_ref, vf_ref, smf_ref, rb_ref, kb_ref, vb_ref, smb_ref,
                kkw_ref, kaw_ref, w0_ref, wup_ref, a0_ref, aup_ref, s0_ref,
                of_ref, ob_ref, sout_ref, sf_ref, sb_ref):
    ci = pl.program_id(2)
    ones_bd = _head_ones(HG)

    @pl.when(ci == 0)
    def _():
        if zero_init:
            sf_ref[...] = jnp.zeros_like(sf_ref)
            sb_ref[...] = jnp.zeros_like(sb_ref)
        else:
            for d, ref in ((0, sf_ref), (1, sb_ref)):
                ref[...] = jnp.zeros_like(ref)
                for h in range(4):
                    sl = slice(h * HEAD_DIM, (h + 1) * HEAD_DIM)
                    ref[sl, sl] = s0_ref[d, h]

    kkw = kkw_ref[...]
    kaw = kaw_ref[...]
    of_ref[...] = _wkv_direction(False, rf_ref[...], kf_ref[...], vf_ref[...], smf_ref[...], kkw, kaw,
                                 w0_ref[0], wup_ref[0], a0_ref[0], aup_ref[0], sf_ref, ones_bd)
    ob_ref[...] = _wkv_direction(True, rb_ref[...], kb_ref[...], vb_ref[...], smb_ref[...], kkw, kaw,
                                 w0_ref[1], wup_ref[1], a0_ref[1], aup_ref[1], sb_ref, ones_bd)

    @pl.when(ci == nc - 1)
    def _():
        for d, ref in ((0, sf_ref), (1, sb_ref)):
            for h in range(4):
                sl = slice(h * HEAD_DIM, (h + 1) * HEAD_DIM)
                sout_ref[d, h] = ref[sl, sl]


def _wkv_scan(rkv, small, n_seq, seq_len, s0, wts):
    t = rkv.shape[1]
    c = WKV_CHUNK
    nc = seq_len // c
    ng = D_MODEL // HG
    zero_init = s0 is None
    if zero_init:
        s0 = jnp.zeros((1, 2, N_HEADS, HEAD_DIM, HEAD_DIM), F32)

    def fwd(which):
        return pl.BlockSpec((None, c, HG), lambda b, g, ci: (which, b * nc + ci, g))

    def bwd(which):
        return pl.BlockSpec((None, c, HG), lambda b, g, ci: (which, b * nc + (nc - 1 - ci), g))

    sm_f = pl.BlockSpec((None, c, 256), lambda b, g, ci: (0, b * nc + ci, 0))
    sm_b = pl.BlockSpec((None, c, 256), lambda b, g, ci: (0, b * nc + (nc - 1 - ci), 0))
    vec = pl.BlockSpec((1, HG), lambda b, g, ci: (0, g))
    vec2 = pl.BlockSpec((2, 1, HG), lambda b, g, ci: (0, 0, g))
    up2 = pl.BlockSpec((2, 64, HG), lambda b, g, ci: (0, 0, g))
    if zero_init:
        s0_spec = pl.BlockSpec((None, 2, 4, HEAD_DIM, HEAD_DIM), lambda b, g, ci: (0, 0, g, 0, 0))
    else:
        s0_spec = pl.BlockSpec((None, 2, 4, HEAD_DIM, HEAD_DIM), lambda b, g, ci: (b, 0, g, 0, 0))
    o_f = pl.BlockSpec((c, HG), lambda b, g, ci: (b * nc + ci, g))
    o_b = pl.BlockSpec((c, HG), lambda b, g, ci: (b * nc + (nc - 1 - ci), g))
    s_out = pl.BlockSpec((None, 2, 4, HEAD_DIM, HEAD_DIM), lambda b, g, ci: (b, 0, g, 0, 0))
    return pl.pallas_call(
        functools.partial(_wkv_kernel, zero_init, nc),
        grid=(n_seq, ng, nc),
        in_specs=[fwd(0), fwd(1), fwd(2), sm_f, bwd(0), bwd(1), bwd(2), sm_b,
                  vec, vec, vec2, up2, vec2, up2, s0_spec],
        out_specs=[o_f, o_b, s_out],
        out_shape=[jax.ShapeDtypeStruct((t, D_MODEL), F32), jax.ShapeDtypeStruct((t, D_MODEL), F32),
                   jax.ShapeDtypeStruct((n_seq, 2, N_HEADS, HEAD_DIM, HEAD_DIM), F32)],
        scratch_shapes=[pltpu.VMEM((HG, HG), F32), pltpu.VMEM((HG, HG), F32)],
        compiler_params=_cparams(("parallel", "parallel", "arbitrary")),
        name="wkv_scan",
    )(rkv, rkv, rkv, small, rkv, rkv, rkv, small,
      wts["k_k"], wts["k_a"], wts["w0"], wts["w_up"], wts["a0"], wts["a_up"], s0)


LRU_CH = 256
LRU_TILE = 384
LRU_SUB = LRU_CH // 8
LRU_LG = LRU_TILE // 128


def _gelu_tanh(x):
    return 0.5 * x * (1.0 + jnp.tanh(0.7978845608028654 * (x + 0.044715 * (x * x * x))))


def _lru_kernel(seq_len, n_rows, lx_ref, lg_ref, cw_ref, cb_ref, wg_ref, bg_ref, lam_ref, h0_ref,
                y_ref, hfin_ref, a_scr, u_scr):
    n_ch = n_rows // LRU_CH
    hfin_ref[...] = jnp.zeros_like(hfin_ref)
    cw = cw_ref[...]
    cb = cb_ref[...]

    def chunk(d, ci, hc):
        rev = d == 1
        start = pl.multiple_of((n_ch - 1 - ci if rev else ci) * LRU_CH, LRU_CH)
        at_seq_start = (start % seq_len) == 0
        at_seq_end = ((start + LRU_CH) % seq_len) == 0
        prev8 = lx_ref[pl.ds(pl.multiple_of(jnp.maximum(start - 8, 0), 8), 8), :]
        nxt8 = lx_ref[pl.ds(pl.multiple_of(jnp.minimum(start + LRU_CH, n_rows - 8), 8), 8), :]
        prev8 = jnp.where(at_seq_start, 0.0, prev8)
        nxt8 = jnp.where(at_seq_end, 0.0, nxt8)
        cur = lx_ref[pl.ds(start, LRU_CH), :]
        ext = jnp.concatenate([prev8, cur, nxt8], axis=0)
        n_ext = LRU_CH + 16
        xm2 = pltpu.roll(ext, 2, axis=0)[8:8 + LRU_CH]
        xm1 = pltpu.roll(ext, 1, axis=0)[8:8 + LRU_CH]
        xp1 = pltpu.roll(ext, n_ext - 1, axis=0)[8:8 + LRU_CH]
        xb = cb + xm2 * cw[0:1] + xm1 * cw[1:2] + cur * cw[2:3] + xp1 * cw[3:4]
        pre = _dot(xb.astype(BF16), wg_ref[d]) + bg_ref[d]
        r_g = _sigmoid(pre[:, :LRU_TILE])
        i_g = _sigmoid(pre[:, LRU_TILE:])
        log_a = (LRU_C * r_g) * (-_softplus(-lam_ref[d]))
        a_val = jnp.exp(log_a)
        th = jnp.tanh(log_a)
        u_val = jnp.sqrt(-2.0 * th / (1.0 - th)) * i_g * xb
        for g in range(LRU_LG):
            a_scr[g] = a_val[:, g * 128:(g + 1) * 128]
            u_scr[g] = u_val[:, g * 128:(g + 1) * 128]

        def step(j, carry):
            jj = (LRU_SUB - 1 - j) if rev else j
            idx = pl.ds(jj, 8, stride=LRU_SUB)
            out = []
            for g in range(LRU_LG):
                h, acc = carry[g]
                a = a_scr[g, idx, :]
                h = a * h + u_scr[g, idx, :]
                acc = a * acc
                u_scr[g, idx, :] = h
                a_scr[g, idx, :] = acc
                out.append((h, acc))
            return tuple(out)

        init = tuple((jnp.zeros((8, 128), F32), jnp.ones((8, 128), F32)) for _ in range(LRU_LG))
        ends = lax.fori_loop(0, LRU_SUB, step, init)
        h_end = jnp.concatenate([e[0] for e in ends], axis=1)
        a_end = jnp.concatenate([e[1] for e in ends], axis=1)
        hc = jnp.where(at_seq_end if rev else at_seq_start, h0_ref[d], hc)
        order = range(7, -1, -1) if rev else range(8)
        for i in order:
            rows = pl.ds(start + i * LRU_SUB, LRU_SUB)
            loc = pl.ds(i * LRU_SUB, LRU_SUB)
            h_loc = jnp.concatenate([u_scr[g, loc, :] for g in range(LRU_LG)], axis=1)
            a_cum = jnp.concatenate([a_scr[g, loc, :] for g in range(LRU_LG)], axis=1)
            h = h_loc + a_cum * hc
            if rev:
                y_ref[rows, :] = (y_ref[rows, :] + h) * _gelu_tanh(lg_ref[rows, :])
            else:
                y_ref[rows, :] = h
            hc = h_end[i:i + 1, :] + a_end[i:i + 1, :] * hc
        seq_i = start // seq_len

        @pl.when(at_seq_start if rev else at_seq_end)
        def _():
            hfin_ref[d, pl.ds(seq_i, 1), :] = hc

        return hc

    for d in range(2):
        lax.fori_loop(0, n_ch, functools.partial(chunk, d), jnp.zeros((1, LRU_TILE), F32))


def _lru_scan(p, n_seq, seq_len, h0, wts):
    t = p.shape[0]
    n_rows = max(seq_len, 8 * LRU_CH)
    nb = t // n_rows
    nt = D_RNN // LRU_TILE
    chain = h0 is not None
    if not chain:
        h0 = jnp.zeros((1, 2, 1, D_RNN), F32)
        h0_spec = pl.BlockSpec((None, 2, 1, LRU_TILE), lambda i, j: (0, 0, 0, j))
    else:
        h0 = h0.reshape(n_seq, 2, 1, D_RNN)
        h0_spec = pl.BlockSpec((None, 2, 1, LRU_TILE), lambda i, j: (i, 0, 0, j))
    lg0 = COL_LG // LRU_TILE
    y, hfin = pl.pallas_call(
        functools.partial(_lru_kernel, seq_len, n_rows),
        grid=(nb, nt),
        in_specs=[
            pl.BlockSpec((n_rows, LRU_TILE), lambda i, j: (i, j)),
            pl.BlockSpec((n_rows, LRU_TILE), lambda i, j: (i, lg0 + j)),
            pl.BlockSpec((CONV_W, LRU_TILE), lambda i, j: (0, j)),
            pl.BlockSpec((1, LRU_TILE), lambda i, j: (0, j)),
            pl.BlockSpec((2, None, LRU_TILE, 2 * LRU_TILE), lambda i, j: (0, j, 0, 0)),
            pl.BlockSpec((2, None, 1, 2 * LRU_TILE), lambda i, j: (0, j, 0, 0)),
            pl.BlockSpec((2, 1, LRU_TILE), lambda i, j: (0, 0, j)),
            h0_spec,
        ],
        out_specs=[
            pl.BlockSpec((n_rows, LRU_TILE), lambda i, j: (i, j)),
            pl.BlockSpec((2, None, 8, LRU_TILE), lambda i, j: (0, i, 0, j)),
        ],
        out_shape=[jax.ShapeDtypeStruct((t, D_RNN), F32), jax.ShapeDtypeStruct((2, nb, 8, D_RNN), F32)],
        scratch_shapes=[pltpu.VMEM((LRU_LG, LRU_CH, 128), F32), pltpu.VMEM((LRU_LG, LRU_CH, 128), F32)],
        compiler_params=_cparams(("parallel", "parallel")),
        name="lru_scan",
    )(p, p, wts["conv_w"], wts["conv_b"], wts["lru_wg"], wts["lru_bg"], wts["lru_lam"], h0)
    return y, hfin


def _lru_gate_weights(wa, ba, wx, bx):
    nt = D_RNN // LRU_TILE
    per = LRU_TILE // LRU_BLOCK
    eye = jnp.eye(per, dtype=F32)

    def tiles(w):
        w = w.reshape(2, nt, per, LRU_BLOCK, LRU_BLOCK)
        bd = jnp.einsum("dtaij,ab->dtaibj", w, eye)
        return bd.reshape(2, nt, LRU_TILE, LRU_TILE)

    wg = jnp.concatenate([tiles(wa), tiles(wx)], axis=-1).astype(BF16)
    bg = jnp.concatenate([ba.reshape(2, nt, 1, LRU_TILE), bx.reshape(2, nt, 1, LRU_TILE)], axis=-1)
    return wg, bg


ROUTER_LANES = 128


def _head_sum(x, ones_bd):
    parts = [_dot_exact_rhs(x[:, g * HG:(g + 1) * HG], ones_bd) for g in range(x.shape[1] // HG)]
    return jnp.concatenate(parts, axis=1)


def _merge_kernel(o0_ref, o1_ref, rkv_ref, sm_ref, ga_ref, gb_ref, yb_ref, x_ref, mod_ref,
                  aup_ref, a0_ref, ka_ref, rk_ref, gup_ref, lng_ref, lnb_ref, proj_ref, bm_ref,
                  wout_ref, g2_ref, rw_ref, rb_ref, x1_ref, h2_ref, comb_ref):
    ones_bd = _head_ones(HG)
    o = o0_ref[...] + o1_ref[...]
    mu = _head_sum(o, ones_bd) * (1.0 / HEAD_DIM)
    oc = o - mu
    var = _head_sum(oc * oc, ones_bd) * (1.0 / HEAD_DIM)
    on = oc * lax.rsqrt(var + GN_EPS)
    r = rkv_ref[0]
    k = rkv_ref[1]
    v = rkv_ref[2]
    sm = sm_ref[...]
    ad = sm[:, 64:128]
    gd = sm[:, 128:256]
    ka = ka_ref[...]
    kd_rk = jnp.zeros_like(k)
    for d in range(2):
        a = _sigmoid(a0_ref[d] + _dot3(ad, aup_ref[d]))
        kd_rk = kd_rk + k * (1.0 + (a - 1.0) * ka) * rk_ref[d]
    bonus = _head_sum(r * kd_rk, ones_bd) * v
    g = _dot(_sigmoid(gd).astype(BF16), gup_ref[...])
    y_a = (on * lng_ref[...] + lnb_ref[...] + bonus) * g
    y_b = _dot(yb_ref[...].astype(BF16), proj_ref[...])
    bm = bm_ref[...]
    gate_a = _sigmoid(ga_ref[...] + bm[:, :D_MODEL])
    gate_b = _sigmoid(gb_ref[...] + bm[:, D_MODEL:])
    y = gate_a * y_a + gate_b * y_b
    m = mod_ref[...]
    gt1 = m[:, 2 * D_MODEL:3 * D_MODEL]
    sh2 = m[:, 3 * D_MODEL:4 * D_MODEL]
    sc2 = m[:, 4 * D_MODEL:5 * D_MODEL]
    x1 = x_ref[...] + gt1 * _dot(y.astype(BF16), wout_ref[...])
    x1_ref[...] = x1
    ms = jnp.mean(x1 * x1, axis=-1, keepdims=True)
    h2 = (x1 * lax.rsqrt(ms + RMS_EPS) * g2_ref[...]) * (1.0 + sc2) + sh2
    h2_ref[...] = h2.astype(BF16)

    logits = _dot3(h2, rw_ref[...]) + rb_ref[...]
    lane = lax.broadcasted_iota(jnp.int32, logits.shape, 1)
    work = logits
    vals, sels = [], []
    for _ in range(TOP_K):
        mx = jnp.max(work, axis=-1, keepdims=True)
        first = jnp.min(jnp.where(work == mx, lane, ROUTER_LANES), axis=-1, keepdims=True)
        sel = lane == first
        vals.append(mx)
        sels.append(sel)
        work = jnp.where(sel, -jnp.inf, work)
    es = [jnp.exp(val - vals[0]) for val in vals]
    inv = 1.0 / (es[0] + es[1] + es[2] + es[3])
    comb = jnp.zeros_like(logits)
    for sel, e in zip(sels, es):
        comb = comb + jnp.where(sel, e * inv, 0.0)
    comb_ref[...] = comb


def _merge(o0, o1, rkv, small, p, yb, x, mod3, mod_base, rows_per_mod, w):
    t = x.shape[0]
    tm = 256

    def row(width, col=0):
        return pl.BlockSpec((tm, width), lambda i: (i, col))

    def full(shape):
        nd = len(shape)
        return pl.BlockSpec(shape, lambda i: (0,) * nd)

    return pl.pallas_call(
        _merge_kernel,
        grid=(t // tm,),
        in_specs=[
            row(D_MODEL), row(D_MODEL),
            pl.BlockSpec((3, tm, D_MODEL), lambda i: (0, i, 0)),
            pl.BlockSpec((None, tm, 256), lambda i: (0, i, 0)),
            row(D_MODEL, COL_GA // D_MODEL), row(D_MODEL, COL_GB // D_MODEL),
            row(D_RNN), row(D_MODEL),
            pl.BlockSpec((None, 1, 6 * D_MODEL), lambda i: (mod_base + (i * tm) // rows_per_mod, 0, 0)),
            full((2, 64, D_MODEL)), full((2, 1, D_MODEL)), full((1, D_MODEL)), full((2, 1, D_MODEL)),
            full((128, D_MODEL)), full((1, D_MODEL)), full((1, D_MODEL)), full((D_RNN, D_MODEL)),
            full((1, 2 * D_MODEL)), full((D_MODEL, D_MODEL)), full((1, D_MODEL)),
            full((D_MODEL, ROUTER_LANES)), full((1, ROUTER_LANES)),
        ],
        out_specs=[row(D_MODEL), row(D_MODEL), row(ROUTER_LANES)],
        out_shape=[jax.ShapeDtypeStruct((t, D_MODEL), F32), jax.ShapeDtypeStruct((t, D_MODEL), BF16),
                   jax.ShapeDtypeStruct((t, ROUTER_LANES), F32)],
        compiler_params=_cparams(("parallel",)),
        name="merge",
    )(o0, o1, rkv, small, p, p, yb, x, mod3,
      w["a_up"], w["a0"], w["k_a"], w["r_k"], w["g_up"], w["ln_g"], w["ln_b"], w["lru_proj"],
      w["b_merge"], w["w_out"], w["norm_ffn_g"], w["router_w"], w["router_b"])


MOE_SUB = 256


def _moe_kernel(n_exp, h_ref, comb_ref, x1_ref, mod_ref, wgu_ref, bgu_ref, wd_ref, bd_ref, fg_ref,
                o_ref, acc_ref):
    e = pl.program_id(1)

    @pl.when(e == 0)
    def _():
        acc_ref[...] = jnp.zeros_like(acc_ref)

    tm = h_ref.shape[0]
    d_exp = wd_ref.shape[0]
    bgu = bgu_ref[...]
    for s in range(tm // MOE_SUB):
        rows = pl.ds(s * MOE_SUB, MOE_SUB)
        h = h_ref[rows, :]
        gate = _dot(h, wgu_ref[:, :d_exp]) + bgu[:, :d_exp]
        up = _dot(h, wgu_ref[:, d_exp:]) + bgu[:, d_exp:]
        gate = jnp.minimum(gate, SWIGLU_LIMIT)
        up = jnp.clip(up, -SWIGLU_LIMIT, SWIGLU_LIMIT)
        act = gate * _sigmoid(SWIGLU_ALPHA * gate) * (up + 1.0)
        y = _dot(act.astype(BF16), wd_ref[...]) + bd_ref[...]
        comb = comb_ref[rows, :]
        lane = lax.broadcasted_iota(jnp.int32, comb.shape, 1)
        ce = jnp.sum(jnp.where(lane == e, comb, 0.0), axis=-1, keepdims=True)
        acc_ref[rows, :] += ce * y

    @pl.when(e == n_exp - 1)
    def _():
        m = mod_ref[...]
        gt2 = m[:, 5 * D_MODEL:6 * D_MODEL]
        x2 = x1_ref[...] + gt2 * acc_ref[...]
        ms = jnp.mean(x2 * x2, axis=-1, keepdims=True)
        o_ref[...] = x2 * lax.rsqrt(ms + RMS_EPS) * fg_ref[...]


def _moe(h2, comb, x1, mod3, mod_base, rows_per_mod, w):
    t = h2.shape[0]
    tm = 1024
    n_exp, _, two_d = w["exp_w_gu"].shape
    d_exp = two_d // 2
    return pl.pallas_call(
        functools.partial(_moe_kernel, n_exp),
        grid=(t // tm, n_exp),
        in_specs=[
            pl.BlockSpec((tm, D_MODEL), lambda i, e: (i, 0)),
            pl.BlockSpec((tm, ROUTER_LANES), lambda i, e: (i, 0)),
            pl.BlockSpec((tm, D_MODEL), lambda i, e: (i, 0)),
            pl.BlockSpec((None, 1, 6 * D_MODEL), lambda i, e: (mod_base + (i * tm) // rows_per_mod, 0, 0)),
            pl.BlockSpec((None, D_MODEL, two_d), lambda i, e: (e, 0, 0)),
            pl.BlockSpec((None, 1, two_d), lambda i, e: (e, 0, 0)),
            pl.BlockSpec((None, d_exp, D_MODEL), lambda i, e: (e, 0, 0)),
            pl.BlockSpec((None, 1, D_MODEL), lambda i, e: (e, 0, 0)),
            pl.BlockSpec((1, D_MODEL), lambda i, e: (0, 0)),
        ],
        out_specs=pl.BlockSpec((tm, D_MODEL), lambda i, e: (i, 0)),
        out_shape=jax.ShapeDtypeStruct((t, D_MODEL), F32),
        scratch_shapes=[pltpu.VMEM((tm, D_MODEL), F32)],
        compiler_params=_cparams(("parallel", "arbitrary")),
        name="moe",
    )(h2, comb, x1, mod3, w["exp_w_gu"], w["exp_b_gu"], w["exp_w_down"], w["exp_b_down"], w["final_norm_g"])


_N_SHIFT = 3328
_PERM = np.concatenate([
    np.arange(_N_SHIFT, _N_SHIFT + 2 * D_RNN),
    np.arange(0, 3 * D_MODEL),
    np.arange(_N_SHIFT + 2 * D_RNN, _N_SHIFT + 2 * D_RNN + 2 * D_MODEL),
    np.arange(3 * D_MODEL, _N_SHIFT),
])


def _stream(x, n_seq, seq_len, mod3, mod_base, rows_per_mod, s_wkv0, s_lru0, grid_mode, w):
    t = n_seq * seq_len
    x2 = x.reshape(t, D_MODEL)
    p = _in_projection(x2, mod3, mod_base, rows_per_mod, w["norm_mix_g"], w["w_in"])
    rkv = _token_shift(p, w["mu"], COL_R, D_MODEL, (D_MODEL,), 3, seq_len, grid_mode)
    small = _token_shift(p, w["mu"], COL_SMALL, 256, (64, 64, 128), 1, seq_len, grid_mode)
    o0, o1, s_wkv = _wkv_scan(rkv, small, n_seq, seq_len, s_wkv0, w)
    yb, s_lru = _lru_scan(p, n_seq, seq_len, s_lru0, w)
    x1, h2, comb = _merge(o0, o1, rkv, small, p, yb, x2, mod3, mod_base, rows_per_mod, w)
    y = _moe(h2, comb, x1, mod3, mod_base, rows_per_mod, w)
    return y.reshape(x.shape), s_wkv, s_lru


def kernel(x_prompt, x_sample, state_wkv, state_lru, c, c_ctx, norm_mix_g, norm_ffn_g, w_ada, b_ada, w_in, shift_mu, wkv_k_k, wkv_k_a, wkv_r_k, wkv_w0, wkv_w_up, wkv_a0, wkv_a_up, wkv_g_up, wkv_ln_g, wkv_ln_b, lru_conv_w, lru_conv_b, lru_wa, lru_ba, lru_wx, lru_bx, lru_lambda, lru_proj, b_merge, w_out, router_w, router_b, exp_w_gu, exp_b_gu, exp_w_down, exp_b_down, final_norm_g):
    n_ctx, seq, _ = x_prompt.shape
    n_dec, dec_seq, _ = x_sample.shape
    assert w_in.shape[0] == 1, "single trunk layer: the final norm is fused into the MoE kernel"
    assert 1 + n_dec <= 8 and dec_seq % GRID_W == 0
    l = 0
    n_exp = router_w.shape[-1]
    cond8 = jnp.concatenate([c_ctx[None, :], c, jnp.zeros((8 - 1 - n_dec, D_MODEL), F32)], axis=0)
    mod3 = _modulation(cond8, w_ada[l], b_ada[l]).reshape(8, 1, 6 * D_MODEL)
    mu_full = jnp.concatenate([shift_mu[l], jnp.zeros((N_IN - _N_SHIFT,), F32)])
    lru_wg, lru_bg = _lru_gate_weights(lru_wa[l], lru_ba[l], lru_wx[l], lru_bx[l])
    router_w_pad = jnp.zeros((D_MODEL, ROUTER_LANES), F32).at[:, :n_exp].set(router_w[l])
    router_b_pad = jnp.full((1, ROUTER_LANES), -1e30, F32).at[0, :n_exp].set(router_b[l])
    w = dict(
        norm_mix_g=norm_mix_g[l],
        w_in=w_in[l][:, _PERM].astype(BF16),
        mu=mu_full[_PERM].reshape(1, N_IN),
        k_k=wkv_k_k[l].reshape(1, D_MODEL), k_a=wkv_k_a[l].reshape(1, D_MODEL),
        w0=wkv_w0[l].reshape(2, 1, D_MODEL), w_up=wkv_w_up[l],
        a0=wkv_a0[l].reshape(2, 1, D_MODEL), a_up=wkv_a_up[l],
        r_k=wkv_r_k[l].reshape(2, 1, D_MODEL), g_up=wkv_g_up[l].astype(BF16),
        ln_g=wkv_ln_g[l].reshape(1, D_MODEL), ln_b=wkv_ln_b[l].reshape(1, D_MODEL),
        conv_w=lru_conv_w[l], conv_b=lru_conv_b[l].reshape(1, D_RNN),
        lru_wg=lru_wg, lru_bg=lru_bg, lru_lam=lru_lambda[l].reshape(2, 1, D_RNN),
        lru_proj=lru_proj[l].astype(BF16), b_merge=b_merge[l].reshape(1, 2 * D_MODEL),
        w_out=w_out[l].astype(BF16), norm_ffn_g=norm_ffn_g[l].reshape(1, D_MODEL),
        router_w=router_w_pad, router_b=router_b_pad,
        exp_w_gu=exp_w_gu[l].astype(BF16), exp_b_gu=exp_b_gu[l].reshape(n_exp, 1, -1),
        exp_w_down=exp_w_down[l].astype(BF16), exp_b_down=exp_b_down[l].reshape(n_exp, 1, D_MODEL),
        final_norm_g=final_norm_g.reshape(1, D_MODEL),
    )
    y_prompt, wkv_ctx, lru_ctx = _stream(x_prompt, n_ctx, seq, mod3, 0, n_ctx * seq, None, None, False, w)
    y_sample, _, _ = _stream(x_sample, n_dec, dec_seq, mod3, 1, dec_seq, state_wkv[:, l], state_lru[:, l], True, w)
    new_lru = lru_ctx.reshape(2, n_ctx, D_RNN).transpose(1, 0, 2)
    return y_prompt, y_sample, wkv_ctx[:, None].astype(state_wkv.dtype), new_lru[:, None].astype(state_lru.dtype)
```

```python
import functools

import numpy as np
import jax
import jax.numpy as jnp
from jax import lax
from jax.experimental import pallas as pl
from jax.experimental.pallas import tpu as pltpu

F32 = jnp.float32
BF16 = jnp.bfloat16

D_MODEL = 1024
HEAD_DIM = 64
N_HEADS = 16
GRID_W = 64
D_RNN = 1536
LRU_BLOCK = 96
CONV_W = 4
LRU_C = 8.0
TOP_K = 4
SWIGLU_LIMIT = 7.0
SWIGLU_ALPHA = 1.702
RMS_EPS = 1e-6
GN_EPS = 64e-5

COL_LX = 0
COL_LG = 1536
COL_R = 3072
COL_K = 4096
COL_V = 5120
COL_GA = 6144
COL_GB = 7168
COL_SMALL = 8192
N_IN = 8448

HG = 256
WKV_CHUNK = 64
WKV_GROUPS_PER_STEP = 4
VMEM_LIMIT = 56 * 1024 * 1024


def _cparams(sem, vmem=VMEM_LIMIT):
    return pltpu.CompilerParams(dimension_semantics=sem, vmem_limit_bytes=vmem)


def _split2(x):
    hi = x.astype(BF16)
    lo = (x - hi.astype(F32)).astype(BF16)
    return hi, lo


def _dot(a, b):
    return jnp.dot(a, b, preferred_element_type=F32)


def _dot3(a, b):
    ah, al = _split2(a)
    bh, bl = _split2(b)
    return _dot(ah, bh) + (_dot(al, bh) + _dot(ah, bl))


def _dot_exact_rhs(a, b_bf16):
    ah, al = _split2(a)
    return _dot(ah, b_bf16) + _dot(al, b_bf16)


def _head_ones(n):
    r = lax.broadcasted_iota(jnp.int32, (n, n), 0) // HEAD_DIM
    c = lax.broadcasted_iota(jnp.int32, (n, n), 1) // HEAD_DIM
    return jnp.where(r == c, 1.0, 0.0).astype(BF16)


def _sigmoid(x):
    return 1.0 / (1.0 + jnp.exp(-x))


def _softplus(x):
    return jnp.maximum(x, 0.0) + jnp.log(1.0 + jnp.exp(-jnp.abs(x)))


def _mod_kernel(c_ref, w_ref, b_ref, o_ref):
    c = c_ref[...]
    s = c * _sigmoid(c)
    o_ref[...] = _dot3(s, w_ref[...]) + b_ref[...]


def _modulation(cond8, w_ada, b_ada):
    n = w_ada.shape[1]
    tn = 1024
    return pl.pallas_call(
        _mod_kernel,
        grid=(n // tn,),
        in_specs=[
            pl.BlockSpec((8, D_MODEL), lambda j: (0, 0)),
            pl.BlockSpec((D_MODEL, tn), lambda j: (0, j)),
            pl.BlockSpec((1, tn), lambda j: (0, j)),
        ],
        out_specs=pl.BlockSpec((8, tn), lambda j: (0, j)),
        out_shape=jax.ShapeDtypeStruct((8, n), F32),
        compiler_params=_cparams(("parallel",)),
        name="modulation",
    )(cond8, w_ada, b_ada.reshape(1, n))


def _inproj_kernel(x_ref, mod_ref, g_ref, w_ref, o_ref, h_ref):
    @pl.when(pl.program_id(1) == 0)
    def _():
        x = x_ref[...]
        ms = jnp.mean(x * x, axis=-1, keepdims=True)
        y = x * lax.rsqrt(ms + RMS_EPS) * g_ref[...]
        m = mod_ref[...]
        sh1 = m[:, 0:D_MODEL]
        sc1 = m[:, D_MODEL:2 * D_MODEL]
        h_ref[...] = (y * (1.0 + sc1) + sh1).astype(BF16)

    o_ref[...] = _dot(h_ref[...], w_ref[...])


def _in_projection(x, mod3, mod_base, rows_per_mod, g, w_bf16):
    t = x.shape[0]
    tm, tn = 1024, 768
    n = w_bf16.shape[1]
    return pl.pallas_call(
        _inproj_kernel,
        grid=(t // tm, n // tn),
        in_specs=[
            pl.BlockSpec((tm, D_MODEL), lambda i, j: (i, 0)),
            pl.BlockSpec((None, 1, 6 * D_MODEL), lambda i, j: (mod_base + (i * tm) // rows_per_mod, 0, 0)),
            pl.BlockSpec((1, D_MODEL), lambda i, j: (0, 0)),
            pl.BlockSpec((D_MODEL, tn), lambda i, j: (0, j)),
        ],
        out_specs=pl.BlockSpec((tm, tn), lambda i, j: (i, j)),
        out_shape=jax.ShapeDtypeStruct((t, n), F32),
        scratch_shapes=[pltpu.VMEM((tm, D_MODEL), BF16)],
        compiler_params=_cparams(("parallel", "arbitrary")),
        name="in_projection",
    )(x, mod3, g.reshape(1, D_MODEL), w_bf16)


def _shift_seq_kernel(seq_len, p_ref, mu_ref, dir_ref, o_ref):
    x = p_ref[...]
    tr = x.shape[0]
    pos = lax.broadcasted_iota(jnp.int32, x.shape, 0) % seq_len
    prev = jnp.where(pos == 0, 0.0, pltpu.roll(x, 1, axis=0))
    nxt = jnp.where(pos == seq_len - 1, 0.0, pltpu.roll(x, tr - 1, axis=0))
    shifted = jnp.where(dir_ref[...] == 0, prev, nxt)
    o_ref[...] = x + mu_ref[...] * (shifted - x)


def _shift_grid_kernel(seq_len, p_ref, hp_ref, hn_ref, mu_ref, dir_ref, o_ref):
    x = p_ref[...]
    tr = x.shape[0]
    i = pl.program_id(0)
    tiles_per_seq = seq_len // tr
    first = (i % tiles_per_seq) == 0
    last = (i % tiles_per_seq) == tiles_per_seq - 1
    col = lax.broadcasted_iota(jnp.int32, x.shape, 0) % GRID_W
    left = jnp.where(col == 0, 0.0, pltpu.roll(x, 1, axis=0))
    right = jnp.where(col == GRID_W - 1, 0.0, pltpu.roll(x, tr - 1, axis=0))
    hp = jnp.where(first, 0.0, hp_ref[...])
    hn = jnp.where(last, 0.0, hn_ref[...])
    up = jnp.concatenate([hp, x[:tr - GRID_W]], axis=0)
    down = jnp.concatenate([x[GRID_W:], hn], axis=0)
    d = dir_ref[...]
    shifted = jnp.where(d == 0, left, jnp.where(d == 1, right, jnp.where(d == 2, up, down)))
    o_ref[...] = x + mu_ref[...] * (shifted - x)


def _dir_codes(widths, n_dirs):
    codes = []
    for w in widths:
        q = w // n_dirs
        codes.append(np.repeat(np.arange(n_dirs, dtype=np.int32), q))
    return np.concatenate(codes)[None, :]


def _token_shift(p, mu_perm, col0, width, group_widths, n_groups, seq_len, grid_mode):
    t = p.shape[0]
    tr = 512
    cb0 = col0 // width
    n_dirs = 4 if grid_mode else 2
    dirs = jnp.asarray(_dir_codes(group_widths, n_dirs))
    main = pl.BlockSpec((tr, width), lambda i, g: (i, cb0 + g))
    mu_spec = pl.BlockSpec((1, width), lambda i, g: (0, cb0 + g))
    dir_spec = pl.BlockSpec((1, width), lambda i, g: (0, 0))
    out_spec = pl.BlockSpec((None, tr, width), lambda i, g: (g, i, 0))
    out_shape = jax.ShapeDtypeStruct((n_groups, t, width), F32)
    if not grid_mode:
        return pl.pallas_call(
            functools.partial(_shift_seq_kernel, seq_len),
            grid=(t // tr, n_groups),
            in_specs=[main, mu_spec, dir_spec],
            out_specs=out_spec,
            out_shape=out_shape,
            compiler_params=_cparams(("parallel", "parallel")),
            name="token_shift_seq",
        )(p, mu_perm, dirs)
    hb = tr // GRID_W
    n_hb = t // GRID_W
    halo_prev = pl.BlockSpec((GRID_W, width), lambda i, g: (jnp.maximum(i * hb - 1, 0), cb0 + g))
    halo_next = pl.BlockSpec((GRID_W, width), lambda i, g: (jnp.minimum((i + 1) * hb, n_hb - 1), cb0 + g))
    return pl.pallas_call(
        functools.partial(_shift_grid_kernel, seq_len),
        grid=(t // tr, n_groups),
        in_specs=[main, halo_prev, halo_next, mu_spec, dir_spec],
        out_specs=out_spec,
        out_shape=out_shape,
        compiler_params=_cparams(("parallel", "parallel")),
        name="token_shift_grid",
    )(p, p, p, mu_perm, dirs)


def _bd_rows(x):
    c = x.shape[0]
    xt = jnp.concatenate([x, x, x, x], axis=0)
    rb = lax.broadcasted_iota(jnp.int32, xt.shape, 0) // c
    lb = lax.broadcasted_iota(jnp.int32, xt.shape, 1) // HEAD_DIM
    return jnp.where(rb == lb, xt, jnp.zeros_like(xt))


def _dot_nt(a, b):
    return lax.dot_general(a, b, (((1,), (1,)), ((), ())), preferred_element_type=F32)


def _dot_tn(a, b):
    return lax.dot_general(a, b, (((0,), (0,)), ((), ())), preferred_element_type=F32)


def _wkv_chains(chains, ones_bd):
    c = chains[0]["r"].shape[0]
    ti = lax.broadcasted_iota(jnp.int32, (c, c), 0)
    si = lax.broadcasted_iota(jnp.int32, (c, c), 1)
    tri = {rev: jnp.where((si >= ti) if rev else (si <= ti), 1.0, 0.0).astype(BF16) for rev in (False, True)}
    t2 = lax.broadcasted_iota(jnp.int32, (c, 4 * c), 0)
    s2 = lax.broadcasted_iota(jnp.int32, (c, 4 * c), 1) % c
    strict = {False: s2 < t2, True: s2 > t2}
    incl = {False: s2 <= t2, True: s2 >= t2}
    eye = jnp.where(s2 == t2, 1.0, 0.0)

    for ch in chains:
        ch["kkr"] = ch["k"] * ch["kk_w"]
    for ch in chains:
        ch["ss"] = _dot_exact_rhs(ch["kkr"] * ch["kkr"], ones_bd)
    for ch in chains:
        ch["wlin"] = ch["w0"] + _dot3(jnp.tanh(ch["sm"][:, 0:64]), ch["wup"])
    for ch in chains:
        ch["a"] = _sigmoid(ch["a0"] + _dot3(ch["sm"][:, 64:128], ch["aup"]))
    for ch in chains:
        logw = -jnp.exp(-_softplus(-ch["wlin"]) - 0.5)
        lh = logw.astype(BF16)
        l1 = logw - lh.astype(F32)
        lm = l1.astype(BF16)
        ll = (l1 - lm.astype(F32)).astype(BF16)
        t = tri[ch["rev"]]
        ch["logw"] = logw
        ch["cum"] = _dot(t, lh) + (_dot(t, lm) + _dot(t, ll))
    for ch in chains:
        rev, cum, k = ch["rev"], ch["cum"], ch["k"]
        kk = ch["kkr"] * lax.rsqrt(jnp.maximum(ch["ss"], 1e-24))
        a = ch["a"]
        kd = k * (1.0 + (a - 1.0) * ch["ka_w"])
        b = kk * a
        ltot = cum[0:1, :] if rev else cum[c - 1:c, :]
        e_neg = jnp.exp(-cum)
        e_tail = jnp.exp(ltot - cum)
        at = -kk * jnp.exp(cum - ch["logw"])
        rt = ch["r"] * jnp.exp(cum)
        ch["ltot"] = ltot
        ch["ar"] = jnp.concatenate([at, rt], axis=0).astype(BF16)
        ch["bt"] = _bd_rows(b * e_neg).astype(BF16)
        ch["kt"] = _bd_rows(kd * e_neg).astype(BF16)
        ch["bk"] = jnp.concatenate([b * e_tail, kd * e_tail], axis=0).astype(BF16)
    for ch in chains:
        ch["m_b"] = _dot_nt(ch["ar"], ch["bt"])
    for ch in chains:
        ch["m_k"] = _dot_nt(ch["ar"], ch["kt"])
    for ch in chains:
        ch["st"] = _dot_nt(ch["ar"], ch["s0"].astype(BF16))
    for ch in chains:
        rev = ch["rev"]
        ch["m_ab"] = jnp.where(strict[rev], ch["m_b"][:c], 0.0)
        ch["n_rb"] = jnp.where(incl[rev], ch["m_b"][c:], 0.0).astype(BF16)
        ch["n_rk"] = jnp.where(incl[rev], ch["m_k"][c:], 0.0).astype(BF16)
        ch["v_bd"] = _bd_rows(ch["v"]).astype(BF16)
    for ch in chains:
        m_ak = jnp.where(strict[ch["rev"]], ch["m_k"][:c], 0.0).astype(BF16)
        ch["rhs"] = ch["st"][:c] + _dot(m_ak, ch["v_bd"])

    blk = 1
    while blk < c:
        tb = t2 // blk
        sb = s2 // blk
        off = {False: (tb % 2 == 1) & (sb == tb - 1), True: (tb % 2 == 0) & (sb == tb + 1)}
        if blk == 1:
            for ch in chains:
                ch["x"] = eye + jnp.where(off[ch["rev"]], ch["m_ab"], 0.0)
        else:
            for ch in chains:
                m_off = jnp.where(off[ch["rev"]], ch["m_ab"], 0.0)
                ch["p1"] = _dot(ch["x"].astype(BF16), _bd_rows(m_off).astype(BF16))
            for ch in chains:
                ch["x"] = ch["x"] + _dot(ch["p1"].astype(BF16), _bd_rows(ch["x"]).astype(BF16))
        blk *= 2

    for ch in chains:
        ch["u"] = _dot(ch["x"].astype(BF16), _bd_rows(ch["rhs"]).astype(BF16))
    for ch in chains:
        u_bd = _bd_rows(ch["u"]).astype(BF16)
        ch["o"] = ch["st"][c:] + _dot(ch["n_rb"], u_bd) + _dot(ch["n_rk"], ch["v_bd"])
    out = []
    for ch in chains:
        uv = jnp.concatenate([ch["u"], ch["v"]], axis=0).astype(BF16)
        upd = _dot_tn(uv, ch["bk"])
        out.append((ch["o"], ch["s0"] * jnp.exp(ch["ltot"]) + jnp.where(ones_bd > 0, upd, 0.0)))
    return out


def _wkv_kernel(zero_init, nc,
                rf_ref, kf_ref, vf_ref, smf_ref, rb_ref, kb_ref, vb_ref, smb_ref,
                kkw_ref, kaw_ref, w0_ref, wup_ref, a0_ref, aup_ref, s0_ref,
                of_ref, ob_ref, sout_ref, sf_ref, sb_ref):
    ci = pl.program_id(2)
    ones_bd = _head_ones(HG)
    n_g = sf_ref.shape[0]

    @pl.when(ci == 0)
    def _():
        sf_ref[...] = jnp.zeros_like(sf_ref)
        sb_ref[...] = jnp.zeros_like(sb_ref)
        if not zero_init:
            for d, ref in ((0, sf_ref), (1, sb_ref)):
                for h in range(4 * n_g):
                    sl = slice((h % 4) * HEAD_DIM, (h % 4 + 1) * HEAD_DIM)
                    ref[h // 4, sl, sl] = s0_ref[d, h]

    smf = smf_ref[...]
    smb = smb_ref[...]
    chains = []
    for g in range(n_g):
        ln = slice(g * HG, (g + 1) * HG)
        common = dict(kk_w=kkw_ref[:, ln], ka_w=kaw_ref[:, ln])
        chains.append(dict(rev=False, r=rf_ref[:, ln], k=kf_ref[:, ln], v=vf_ref[:, ln], sm=smf,
                           w0=w0_ref[0, :, ln], wup=wup_ref[0, :, ln], a0=a0_ref[0, :, ln],
                           aup=aup_ref[0, :, ln], s0=sf_ref[g], **common))
        chains.append(dict(rev=True, r=rb_ref[:, ln], k=kb_ref[:, ln], v=vb_ref[:, ln], sm=smb,
                           w0=w0_ref[1, :, ln], wup=wup_ref[1, :, ln], a0=a0_ref[1, :, ln],
                           aup=aup_ref[1, :, ln], s0=sb_ref[g], **common))
    results = _wkv_chains(chains, ones_bd)
    for g in range(n_g):
        ln = slice(g * HG, (g + 1) * HG)
        (o_f, s_f), (o_b, s_b) = results[2 * g], results[2 * g + 1]
        of_ref[:, ln] = o_f
        ob_ref[:, ln] = o_b
        sf_ref[g] = s_f
        sb_ref[g] = s_b

    @pl.when(ci == nc - 1)
    def _():
        for d, ref in ((0, sf_ref), (1, sb_ref)):
            for h in range(4 * n_g):
                sl = slice((h % 4) * HEAD_DIM, (h % 4 + 1) * HEAD_DIM)
                sout_ref[d, h] = ref[h // 4, sl, sl]


def _wkv_scan(rkv, small, n_seq, seq_len, s0, wts):
    t = rkv.shape[1]
    c = WKV_CHUNK
    nc = seq_len // c
    gs = WKV_GROUPS_PER_STEP
    wl = gs * HG
    ng = D_MODEL // wl
    zero_init = s0 is None
    if zero_init:
        s0 = jnp.zeros((1, 2, N_HEADS, HEAD_DIM, HEAD_DIM), F32)

    def fwd(which):
        return pl.BlockSpec((None, c, wl), lambda b, g, ci: (which, b * nc + ci, g))

    def bwd(which):
        return pl.BlockSpec((None, c, wl), lambda b, g, ci: (which, b * nc + (nc - 1 - ci), g))

    sm_f = pl.BlockSpec((None, c, 256), lambda b, g, ci: (0, b * nc + ci, 0))
    sm_b = pl.BlockSpec((None, c, 256), lambda b, g, ci: (0, b * nc + (nc - 1 - ci), 0))
    vec = pl.BlockSpec((1, wl), lambda b, g, ci: (0, g))
    vec2 = pl.BlockSpec((2, 1, wl), lambda b, g, ci: (0, 0, g))
    up2 = pl.BlockSpec((2, 64, wl), lambda b, g, ci: (0, 0, g))
    if zero_init:
        s0_spec = pl.BlockSpec((None, 2, 4 * gs, HEAD_DIM, HEAD_DIM), lambda b, g, ci: (0, 0, g, 0, 0))
    else:
        s0_spec = pl.BlockSpec((None, 2, 4 * gs, HEAD_DIM, HEAD_DIM), lambda b, g, ci: (b, 0, g, 0, 0))
    o_f = pl.BlockSpec((c, wl), lambda b, g, ci: (b * nc + ci, g))
    o_b = pl.BlockSpec((c, wl), lambda b, g, ci: (b * nc + (nc - 1 - ci), g))
    s_out = pl.BlockSpec((None, 2, 4 * gs, HEAD_DIM, HEAD_DIM), lambda b, g, ci: (b, 0, g, 0, 0))
    return pl.pallas_call(
        functools.partial(_wkv_kernel, zero_init, nc),
        grid=(n_seq, ng, nc),
        in_specs=[fwd(0), fwd(1), fwd(2), sm_f, bwd(0), bwd(1), bwd(2), sm_b,
                  vec, vec, vec2, up2, vec2, up2, s0_spec],
        out_specs=[o_f, o_b, s_out],
        out_shape=[jax.ShapeDtypeStruct((t, D_MODEL), F32), jax.ShapeDtypeStruct((t, D_MODEL), F32),
                   jax.ShapeDtypeStruct((n_seq, 2, N_HEADS, HEAD_DIM, HEAD_DIM), F32)],
        scratch_shapes=[pltpu.VMEM((gs, HG, HG), F32), pltpu.VMEM((gs, HG, HG), F32)],
        compiler_params=_cparams(("parallel", "parallel", "arbitrary")),
        name="wkv_scan",
    )(rkv, rkv, rkv, small, rkv, rkv, rkv, small,
      wts["k_k"], wts["k_a"], wts["w0"], wts["w_up"], wts["a0"], wts["a_up"], s0)


LRU_CH = 256
LRU_TILE = 384


def _gelu_tanh(x):
    return 0.5 * x * (1.0 + jnp.tanh(0.7978845608028654 * (x + 0.044715 * (x * x * x))))


def _lru_kernel(seq_len, n_rows, lx_ref, lg_ref, cw_ref, cb_ref, wg_ref, bg_ref, lam_ref, h0_ref,
                y_ref, hfin_ref):
    n_ch = n_rows // LRU_CH
    hfin_ref[...] = jnp.zeros_like(hfin_ref)
    cw = cw_ref[...]
    cb = cb_ref[...]

    def chunk(d, ci, hc):
        rev = d == 1
        start = pl.multiple_of((n_ch - 1 - ci if rev else ci) * LRU_CH, LRU_CH)
        at_seq_start = (start % seq_len) == 0
        at_seq_end = ((start + LRU_CH) % seq_len) == 0
        prev8 = lx_ref[pl.ds(pl.multiple_of(jnp.maximum(start - 8, 0), 8), 8), :]
        nxt8 = lx_ref[pl.ds(pl.multiple_of(jnp.minimum(start + LRU_CH, n_rows - 8), 8), 8), :]
        prev8 = jnp.where(at_seq_start, 0.0, prev8)
        nxt8 = jnp.where(at_seq_end, 0.0, nxt8)
        cur = lx_ref[pl.ds(start, LRU_CH), :]
        ext = jnp.concatenate([prev8, cur, nxt8], axis=0)
        n_ext = LRU_CH + 16
        xm2 = pltpu.roll(ext, 2, axis=0)[8:8 + LRU_CH]
        xm1 = pltpu.roll(ext, 1, axis=0)[8:8 + LRU_CH]
        xp1 = pltpu.roll(ext, n_ext - 1, axis=0)[8:8 + LRU_CH]
        xb = cb + xm2 * cw[0:1] + xm1 * cw[1:2] + cur * cw[2:3] + xp1 * cw[3:4]
        pre = _dot(xb.astype(BF16), wg_ref[d]) + bg_ref[d]
        r_g = _sigmoid(pre[:, :LRU_TILE])
        i_g = _sigmoid(pre[:, LRU_TILE:])
        log_a = (LRU_C * r_g) * (-_softplus(-lam_ref[d]))
        a_val = jnp.exp(log_a)
        th = jnp.tanh(log_a)
        u_val = jnp.sqrt(-2.0 * th / (1.0 - th)) * i_g * xb

        sub = lax.broadcasted_iota(jnp.int32, (LRU_CH, LRU_TILE), 0) % 8
        for s in (1, 2, 4):
            shift = LRU_CH - s if rev else s
            valid = (sub < 8 - s) if rev else (sub >= s)
            a_sh = pltpu.roll(a_val, shift, axis=0)
            u_sh = pltpu.roll(u_val, shift, axis=0)
            u_val = jnp.where(valid, a_val * u_sh + u_val, u_val)
            a_val = jnp.where(valid, a_val * a_sh, a_val)
        hc = jnp.where(at_seq_end if rev else at_seq_start, h0_ref[d], hc)
        n_grp = LRU_CH // 8
        pieces = [None] * n_grp
        for j in (range(n_grp - 1, -1, -1) if rev else range(n_grp)):
            h = u_val[8 * j:8 * j + 8] + a_val[8 * j:8 * j + 8] * hc
            pieces[j] = h
            hc = h[0:1] if rev else h[7:8]
        h_all = jnp.concatenate(pieces, axis=0)
        rows = pl.ds(start, LRU_CH)
        if rev:
            y_ref[rows, :] = (y_ref[rows, :] + h_all) * _gelu_tanh(lg_ref[rows, :])
        else:
            y_ref[rows, :] = h_all
        seq_i = start // seq_len

        @pl.when(at_seq_start if rev else at_seq_end)
        def _():
            hfin_ref[d, pl.ds(seq_i, 1), :] = hc

        return hc

    for d in range(2):
        lax.fori_loop(0, n_ch, functools.partial(chunk, d), jnp.zeros((1, LRU_TILE), F32))


def _lru_scan(p, n_seq, seq_len, h0, wts):
    t = p.shape[0]
    n_rows = max(seq_len, 8 * LRU_CH)
    nb = t // n_rows
    nt = D_RNN // LRU_TILE
    chain = h0 is not None
    if not chain:
        h0 = jnp.zeros((1, 2, 1, D_RNN), F32)
        h0_spec = pl.BlockSpec((None, 2, 1, LRU_TILE), lambda i, j: (0, 0, 0, j))
    else:
        h0 = h0.reshape(n_seq, 2, 1, D_RNN)
        h0_spec = pl.BlockSpec((None, 2, 1, LRU_TILE), lambda i, j: (i, 0, 0, j))
    lg0 = COL_LG // LRU_TILE
    y, hfin = pl.pallas_call(
        functools.partial(_lru_kernel, seq_len, n_rows),
        grid=(nb, nt),
        in_specs=[
            pl.BlockSpec((n_rows, LRU_TILE), lambda i, j: (i, j)),
            pl.BlockSpec((n_rows, LRU_TILE), lambda i, j: (i, lg0 + j)),
            pl.BlockSpec((CONV_W, LRU_TILE), lambda i, j: (0, j)),
            pl.BlockSpec((1, LRU_TILE), lambda i, j: (0, j)),
            pl.BlockSpec((2, None, LRU_TILE, 2 * LRU_TILE), lambda i, j: (0, j, 0, 0)),
            pl.BlockSpec((2, None, 1, 2 * LRU_TILE), lambda i, j: (0, j, 0, 0)),
            pl.BlockSpec((2, 1, LRU_TILE), lambda i, j: (0, 0, j)),
            h0_spec,
        ],
        out_specs=[
            pl.BlockSpec((n_rows, LRU_TILE), lambda i, j: (i, j)),
            pl.BlockSpec((2, None, 8, LRU_TILE), lambda i, j: (0, i, 0, j)),
        ],
        out_shape=[jax.ShapeDtypeStruct((t, D_RNN), F32), jax.ShapeDtypeStruct((2, nb, 8, D_RNN), F32)],
        compiler_params=_cparams(("parallel", "parallel")),
        name="lru_scan",
    )(p, p, wts["conv_w"], wts["conv_b"], wts["lru_wg"], wts["lru_bg"], wts["lru_lam"], h0)
    return y, hfin


def _lru_gate_weights(wa, ba, wx, bx):
    nt = D_RNN // LRU_TILE
    per = LRU_TILE // LRU_BLOCK
    eye = jnp.eye(per, dtype=F32)

    def tiles(w):
        w = w.reshape(2, nt, per, LRU_BLOCK, LRU_BLOCK)
        bd = jnp.einsum("dtaij,ab->dtaibj", w, eye)
        return bd.reshape(2, nt, LRU_TILE, LRU_TILE)

    wg = jnp.concatenate([tiles(wa), tiles(wx)], axis=-1).astype(BF16)
    bg = jnp.concatenate([ba.reshape(2, nt, 1, LRU_TILE), bx.reshape(2, nt, 1, LRU_TILE)], axis=-1)
    return wg, bg


ROUTER_LANES = 128


def _head_sum(x, ones_bd):
    parts = [_dot_exact_rhs(x[:, g * HG:(g + 1) * HG], ones_bd) for g in range(x.shape[1] // HG)]
    return jnp.concatenate(parts, axis=1)


def _merge_kernel(o0_ref, o1_ref, rkv_ref, sm_ref, ga_ref, gb_ref, yb_ref, x_ref, mod_ref,
                  aup_ref, a0_ref, ka_ref, rk_ref, gup_ref, lng_ref, lnb_ref, proj_ref, bm_ref,
                  wout_ref, g2_ref, rw_ref, rb_ref, x1_ref, h2_ref, comb_ref):
    ones_bd = _head_ones(HG)
    o = o0_ref[...] + o1_ref[...]
    mu = _head_sum(o, ones_bd) * (1.0 / HEAD_DIM)
    oc = o - mu
    var = _head_sum(oc * oc, ones_bd) * (1.0 / HEAD_DIM)
    on = oc * lax.rsqrt(var + GN_EPS)
    r = rkv_ref[0]
    k = rkv_ref[1]
    v = rkv_ref[2]
    sm = sm_ref[...]
    ad = sm[:, 64:128]
    gd = sm[:, 128:256]
    ka = ka_ref[...]
    kd_rk = jnp.zeros_like(k)
    for d in range(2):
        a = _sigmoid(a0_ref[d] + _dot3(ad, aup_ref[d]))
        kd_rk = kd_rk + k * (1.0 + (a - 1.0) * ka) * rk_ref[d]
    bonus = _head_sum(r * kd_rk, ones_bd) * v
    g = _dot(_sigmoid(gd).astype(BF16), gup_ref[...])
    y_a = (on * lng_ref[...] + lnb_ref[...] + bonus) * g
    y_b = _dot(yb_ref[...].astype(BF16), proj_ref[...])
    bm = bm_ref[...]
    gate_a = _sigmoid(ga_ref[...] + bm[:, :D_MODEL])
    gate_b = _sigmoid(gb_ref[...] + bm[:, D_MODEL:])
    y = gate_a * y_a + gate_b * y_b
    m = mod_ref[...]
    gt1 = m[:, 2 * D_MODEL:3 * D_MODEL]
    sh2 = m[:, 3 * D_MODEL:4 * D_MODEL]
    sc2 = m[:, 4 * D_MODEL:5 * D_MODEL]
    x1 = x_ref[...] + gt1 * _dot(y.astype(BF16), wout_ref[...])
    x1_ref[...] = x1
    ms = jnp.mean(x1 * x1, axis=-1, keepdims=True)
    h2 = (x1 * lax.rsqrt(ms + RMS_EPS) * g2_ref[...]) * (1.0 + sc2) + sh2
    h2_ref[...] = h2.astype(BF16)

    logits = _dot3(h2, rw_ref[...]) + rb_ref[...]
    lane = lax.broadcasted_iota(jnp.int32, logits.shape, 1)
    work = logits
    vals, sels = [], []
    for _ in range(TOP_K):
        mx = jnp.max(work, axis=-1, keepdims=True)
        first = jnp.min(jnp.where(work == mx, lane, ROUTER_LANES), axis=-1, keepdims=True)
        sel = lane == first
        vals.append(mx)
        sels.append(sel)
        work = jnp.where(sel, -jnp.inf, work)
    es = [jnp.exp(val - vals[0]) for val in vals]
    inv = 1.0 / (es[0] + es[1] + es[2] + es[3])
    comb = jnp.zeros_like(logits)
    for sel, e in zip(sels, es):
        comb = comb + jnp.where(sel, e * inv, 0.0)
    comb_ref[...] = comb


def _merge(o0, o1, rkv, small, p, yb, x, mod3, mod_base, rows_per_mod, w):
    t = x.shape[0]
    tm = 256

    def row(width, col=0):
        return pl.BlockSpec((tm, width), lambda i: (i, col))

    def full(shape):
        nd = len(shape)
        return pl.BlockSpec(shape, lambda i: (0,) * nd)

    return pl.pallas_call(
        _merge_kernel,
        grid=(t // tm,),
        in_specs=[
            row(D_MODEL), row(D_MODEL),
            pl.BlockSpec((3, tm, D_MODEL), lambda i: (0, i, 0)),
            pl.BlockSpec((None, tm, 256), lambda i: (0, i, 0)),
            row(D_MODEL, COL_GA // D_MODEL), row(D_MODEL, COL_GB // D_MODEL),
            row(D_RNN), row(D_MODEL),
            pl.BlockSpec((None, 1, 6 * D_MODEL), lambda i: (mod_base + (i * tm) // rows_per_mod, 0, 0)),
            full((2, 64, D_MODEL)), full((2, 1, D_MODEL)), full((1, D_MODEL)), full((2, 1, D_MODEL)),
            full((128, D_MODEL)), full((1, D_MODEL)), full((1, D_MODEL)), full((D_RNN, D_MODEL)),
            full((1, 2 * D_MODEL)), full((D_MODEL, D_MODEL)), full((1, D_MODEL)),
            full((D_MODEL, ROUTER_LANES)), full((1, ROUTER_LANES)),
        ],
        out_specs=[row(D_MODEL), row(D_MODEL), row(ROUTER_LANES)],
        out_shape=[jax.ShapeDtypeStruct((t, D_MODEL), F32), jax.ShapeDtypeStruct((t, D_MODEL), BF16),
                   jax.ShapeDtypeStruct((t, ROUTER_LANES), F32)],
        compiler_params=_cparams(("parallel",)),
        name="merge",
    )(o0, o1, rkv, small, p, p, yb, x, mod3,
      w["a_up"], w["a0"], w["k_a"], w["r_k"], w["g_up"], w["ln_g"], w["ln_b"], w["lru_proj"],
      w["b_merge"], w["w_out"], w["norm_ffn_g"], w["router_w"], w["router_b"])


MOE_SUB = 256


def _moe_kernel(n_exp, h_ref, comb_ref, x1_ref, mod_ref, wgu_ref, bgu_ref, wd_ref, bd_ref, fg_ref,
                o_ref, acc_ref):
    e = pl.program_id(1)

    @pl.when(e == 0)
    def _():
        acc_ref[...] = jnp.zeros_like(acc_ref)

    tm = h_ref.shape[0]
    d_exp = wd_ref.shape[0]
    bgu = bgu_ref[...]
    for s in range(tm // MOE_SUB):
        rows = pl.ds(s * MOE_SUB, MOE_SUB)
        h = h_ref[rows, :]
        gate = _dot(h, wgu_ref[:, :d_exp]) + bgu[:, :d_exp]
        up = _dot(h, wgu_ref[:, d_exp:]) + bgu[:, d_exp:]
        gate = jnp.minimum(gate, SWIGLU_LIMIT)
        up = jnp.clip(up, -SWIGLU_LIMIT, SWIGLU_LIMIT)
        act = gate * _sigmoid(SWIGLU_ALPHA * gate) * (up + 1.0)
        y = _dot(act.astype(BF16), wd_ref[...]) + bd_ref[...]
        comb = comb_ref[rows, :]
        lane = lax.broadcasted_iota(jnp.int32, comb.shape, 1)
        ce = jnp.sum(jnp.where(lane == e, comb, 0.0), axis=-1, keepdims=True)
        acc_ref[rows, :] += ce * y

    @pl.when(e == n_exp - 1)
    def _():
        m = mod_ref[...]
        gt2 = m[:, 5 * D_MODEL:6 * D_MODEL]
        x2 = x1_ref[...] + gt2 * acc_ref[...]
        ms = jnp.mean(x2 * x2, axis=-1, keepdims=True)
        o_ref[...] = x2 * lax.rsqrt(ms + RMS_EPS) * fg_ref[...]


def _moe(h2, comb, x1, mod3, mod_base, rows_per_mod, w):
    t = h2.shape[0]
    tm = 1024
    n_exp, _, two_d = w["exp_w_gu"].shape
    d_exp = two_d // 2
    return pl.pallas_call(
        functools.partial(_moe_kernel, n_exp),
        grid=(t // tm, n_exp),
        in_specs=[
            pl.BlockSpec((tm, D_MODEL), lambda i, e: (i, 0)),
            pl.BlockSpec((tm, ROUTER_LANES), lambda i, e: (i, 0)),
            pl.BlockSpec((tm, D_MODEL), lambda i, e: (i, 0)),
            pl.BlockSpec((None, 1, 6 * D_MODEL), lambda i, e: (mod_base + (i * tm) // rows_per_mod, 0, 0)),
            pl.BlockSpec((None, D_MODEL, two_d), lambda i, e: (e, 0, 0)),
            pl.BlockSpec((None, 1, two_d), lambda i, e: (e, 0, 0)),
            pl.BlockSpec((None, d_exp, D_MODEL), lambda i, e: (e, 0, 0)),
            pl.BlockSpec((None, 1, D_MODEL), lambda i, e: (e, 0, 0)),
            pl.BlockSpec((1, D_MODEL), lambda i, e: (0, 0)),
        ],
        out_specs=pl.BlockSpec((tm, D_MODEL), lambda i, e: (i, 0)),
        out_shape=jax.ShapeDtypeStruct((t, D_MODEL), F32),
        scratch_shapes=[pltpu.VMEM((tm, D_MODEL), F32)],
        compiler_params=_cparams(("parallel", "arbitrary")),
        name="moe",
    )(h2, comb, x1, mod3, w["exp_w_gu"], w["exp_b_gu"], w["exp_w_down"], w["exp_b_down"], w["final_norm_g"])


_N_SHIFT = 3328
_PERM = np.concatenate([
    np.arange(_N_SHIFT, _N_SHIFT + 2 * D_RNN),
    np.arange(0, 3 * D_MODEL),
    np.arange(_N_SHIFT + 2 * D_RNN, _N_SHIFT + 2 * D_RNN + 2 * D_MODEL),
    np.arange(3 * D_MODEL, _N_SHIFT),
])


def _stream(x, n_seq, seq_len, mod3, mod_base, rows_per_mod, s_wkv0, s_lru0, grid_mode, w):
    t = n_seq * seq_len
    x2 = x.reshape(t, D_MODEL)
    p = _in_projection(x2, mod3, mod_base, rows_per_mod, w["norm_mix_g"], w["w_in"])
    rkv = _token_shift(p, w["mu"], COL_R, D_MODEL, (D_MODEL,), 3, seq_len, grid_mode)
    small = _token_shift(p, w["mu"], COL_SMALL, 256, (64, 64, 128), 1, seq_len, grid_mode)
    o0, o1, s_wkv = _wkv_scan(rkv, small, n_seq, seq_len, s_wkv0, w)
    yb, s_lru = _lru_scan(p, n_seq, seq_len, s_lru0, w)
    x1, h2, comb = _merge(o0, o1, rkv, small, p, yb, x2, mod3, mod_base, rows_per_mod, w)
    y = _moe(h2, comb, x1, mod3, mod_base, rows_per_mod, w)
    return y.reshape(x.shape), s_wkv, s_lru


def kernel(x_prompt, x_sample, state_wkv, state_lru, c, c_ctx, norm_mix_g, norm_ffn_g, w_ada, b_ada, w_in, shift_mu, wkv_k_k, wkv_k_a, wkv_r_k, wkv_w0, wkv_w_up, wkv_a0, wkv_a_up, wkv_g_up, wkv_ln_g, wkv_ln_b, lru_conv_w, lru_conv_b, lru_wa, lru_ba, lru_wx, lru_bx, lru_lambda, lru_proj, b_merge, w_out, router_w, router_b, exp_w_gu, exp_b_gu, exp_w_down, exp_b_down, final_norm_g):
    n_ctx, seq, _ = x_prompt.shape
    n_dec, dec_seq, _ = x_sample.shape
    assert w_in.shape[0] == 1, "single trunk layer: the final norm is fused into the MoE kernel"
    assert 1 + n_dec <= 8 and dec_seq % GRID_W == 0
    l = 0
    n_exp = router_w.shape[-1]
    cond8 = jnp.concatenate([c_ctx[None, :], c, jnp.zeros((8 - 1 - n_dec, D_MODEL), F32)], axis=0)
    mod3 = _modulation(cond8, w_ada[l], b_ada[l]).reshape(8, 1, 6 * D_MODEL)
    mu_full = jnp.concatenate([shift_mu[l], jnp.zeros((N_IN - _N_SHIFT,), F32)])
    lru_wg, lru_bg = _lru_gate_weights(lru_wa[l], lru_ba[l], lru_wx[l], lru_bx[l])
    router_w_pad = jnp.zeros((D_MODEL, ROUTER_LANES), F32).at[:, :n_exp].set(router_w[l])
    router_b_pad = jnp.full((1, ROUTER_LANES), -1e30, F32).at[0, :n_exp].set(router_b[l])
    w = dict(
        norm_mix_g=norm_mix_g[l],
        w_in=w_in[l][:, _PERM].astype(BF16),
        mu=mu_full[_PERM].reshape(1, N_IN),
        k_k=wkv_k_k[l].reshape(1, D_MODEL), k_a=wkv_k_a[l].reshape(1, D_MODEL),
        w0=wkv_w0[l].reshape(2, 1, D_MODEL), w_up=wkv_w_up[l],
        a0=wkv_a0[l].reshape(2, 1, D_MODEL), a_up=wkv_a_up[l],
        r_k=wkv_r_k[l].reshape(2, 1, D_MODEL), g_up=wkv_g_up[l].astype(BF16),
        ln_g=wkv_ln_g[l].reshape(1, D_MODEL), ln_b=wkv_ln_b[l].reshape(1, D_MODEL),
        conv_w=lru_conv_w[l], conv_b=lru_conv_b[l].reshape(1, D_RNN),
        lru_wg=lru_wg, lru_bg=lru_bg, lru_lam=lru_lambda[l].reshape(2, 1, D_RNN),
        lru_proj=lru_proj[l].astype(BF16), b_merge=b_merge[l].reshape(1, 2 * D_MODEL),
        w_out=w_out[l].astype(BF16), norm_ffn_g=norm_ffn_g[l].reshape(1, D_MODEL),
        router_w=router_w_pad, router_b=router_b_pad,
        exp_w_gu=exp_w_gu[l].astype(BF16), exp_b_gu=exp_b_gu[l].reshape(n_exp, 1, -1),
        exp_w_down=exp_w_down[l].astype(BF16), exp_b_down=exp_b_down[l].reshape(n_exp, 1, D_MODEL),
        final_norm_g=final_norm_g.reshape(1, D_MODEL),
    )
    y_prompt, wkv_ctx, lru_ctx = _stream(x_prompt, n_ctx, seq, mod3, 0, n_ctx * seq, None, None, False, w)
    y_sample, _, _ = _stream(x_sample, n_dec, dec_seq, mod3, 1, dec_seq, state_wkv[:, l], state_lru[:, l], True, w)
    new_lru = lru_ctx.reshape(2, n_ctx, D_RNN).transpose(1, 0, 2)
    return y_prompt, y_sample, wkv_ctx[:, None].astype(state_wkv.dtype), new_lru[:, None].astype(state_lru.dtype)
```

```python
import functools

import numpy as np
import jax
import jax.numpy as jnp
from jax import lax
from jax.experimental import pallas as pl
from jax.experimental.pallas import tpu as pltpu
from jax.experimental.pallas import tpu_sc as plsc

F32 = jnp.float32
BF16 = jnp.bfloat16

D_MODEL = 1024
HEAD_DIM = 64
N_HEADS = 16
GRID_W = 64
D_RNN = 1536
LRU_BLOCK = 96
CONV_W = 4
LRU_C = 8.0
TOP_K = 4
SWIGLU_LIMIT = 7.0
SWIGLU_ALPHA = 1.702
RMS_EPS = 1e-6
GN_EPS = 64e-5

COL_LX = 0
COL_LG = 1536
COL_R = 3072
COL_K = 4096
COL_V = 5120
COL_GA = 6144
COL_GB = 7168
COL_SMALL = 8192
N_IN = 8448

HG = 256
WKV_CHUNK = 64
WKV_GROUPS_PER_STEP = 4
VMEM_LIMIT = 56 * 1024 * 1024


def _cparams(sem, vmem=VMEM_LIMIT):
    return pltpu.CompilerParams(dimension_semantics=sem, vmem_limit_bytes=vmem)


def _split2(x):
    hi = x.astype(BF16)
    lo = (x - hi.astype(F32)).astype(BF16)
    return hi, lo


def _dot(a, b):
    return jnp.dot(a, b, preferred_element_type=F32)


def _dot3(a, b):
    ah, al = _split2(a)
    bh, bl = _split2(b)
    return _dot(ah, bh) + (_dot(al, bh) + _dot(ah, bl))


def _dot_exact_rhs(a, b_bf16):
    ah, al = _split2(a)
    return _dot(ah, b_bf16) + _dot(al, b_bf16)


def _head_ones(n):
    r = lax.broadcasted_iota(jnp.int32, (n, n), 0) // HEAD_DIM
    c = lax.broadcasted_iota(jnp.int32, (n, n), 1) // HEAD_DIM
    return jnp.where(r == c, 1.0, 0.0).astype(BF16)


def _sigmoid(x):
    return 1.0 / (1.0 + jnp.exp(-x))


def _softplus(x):
    return jnp.maximum(x, 0.0) + jnp.log(1.0 + jnp.exp(-jnp.abs(x)))


def _mod_kernel(c_ref, w_ref, b_ref, o_ref):
    c = c_ref[...]
    s = c * _sigmoid(c)
    o_ref[...] = _dot3(s, w_ref[...]) + b_ref[...]


def _modulation(cond8, w_ada, b_ada):
    n = w_ada.shape[1]
    tn = 1024
    return pl.pallas_call(
        _mod_kernel,
        grid=(n // tn,),
        in_specs=[
            pl.BlockSpec((8, D_MODEL), lambda j: (0, 0)),
            pl.BlockSpec((D_MODEL, tn), lambda j: (0, j)),
            pl.BlockSpec((1, tn), lambda j: (0, j)),
        ],
        out_specs=pl.BlockSpec((8, tn), lambda j: (0, j)),
        out_shape=jax.ShapeDtypeStruct((8, n), F32),
        compiler_params=_cparams(("parallel",)),
        name="modulation",
    )(cond8, w_ada, b_ada.reshape(1, n))


def _inproj_kernel(x_ref, mod_ref, g_ref, w_ref, o_ref, h_ref):
    @pl.when(pl.program_id(1) == 0)
    def _():
        x = x_ref[...]
        ms = jnp.mean(x * x, axis=-1, keepdims=True)
        y = x * lax.rsqrt(ms + RMS_EPS) * g_ref[...]
        m = mod_ref[...]
        sh1 = m[:, 0:D_MODEL]
        sc1 = m[:, D_MODEL:2 * D_MODEL]
        h_ref[...] = (y * (1.0 + sc1) + sh1).astype(BF16)

    o_ref[...] = _dot(h_ref[...], w_ref[...])


def _in_projection(x, mod3, mod_base, rows_per_mod, g, w_bf16):
    t = x.shape[0]
    tm, tn = 1024, 768
    n = w_bf16.shape[1]
    return pl.pallas_call(
        _inproj_kernel,
        grid=(t // tm, n // tn),
        in_specs=[
            pl.BlockSpec((tm, D_MODEL), lambda i, j: (i, 0)),
            pl.BlockSpec((None, 1, 6 * D_MODEL), lambda i, j: (mod_base + (i * tm) // rows_per_mod, 0, 0)),
            pl.BlockSpec((1, D_MODEL), lambda i, j: (0, 0)),
            pl.BlockSpec((D_MODEL, tn), lambda i, j: (0, j)),
        ],
        out_specs=pl.BlockSpec((tm, tn), lambda i, j: (i, j)),
        out_shape=jax.ShapeDtypeStruct((t, n), F32),
        scratch_shapes=[pltpu.VMEM((tm, D_MODEL), BF16)],
        compiler_params=_cparams(("parallel", "arbitrary")),
        name="in_projection",
    )(x, mod3, g.reshape(1, D_MODEL), w_bf16)


def _shift_seq_kernel(seq_len, p_ref, mu_ref, dir_ref, o_ref):
    x = p_ref[...]
    tr = x.shape[0]
    pos = lax.broadcasted_iota(jnp.int32, x.shape, 0) % seq_len
    prev = jnp.where(pos == 0, 0.0, pltpu.roll(x, 1, axis=0))
    nxt = jnp.where(pos == seq_len - 1, 0.0, pltpu.roll(x, tr - 1, axis=0))
    shifted = jnp.where(dir_ref[...] == 0, prev, nxt)
    o_ref[...] = x + mu_ref[...] * (shifted - x)


def _shift_grid_kernel(seq_len, p_ref, hp_ref, hn_ref, mu_ref, dir_ref, o_ref):
    x = p_ref[...]
    tr = x.shape[0]
    i = pl.program_id(0)
    tiles_per_seq = seq_len // tr
    first = (i % tiles_per_seq) == 0
    last = (i % tiles_per_seq) == tiles_per_seq - 1
    col = lax.broadcasted_iota(jnp.int32, x.shape, 0) % GRID_W
    left = jnp.where(col == 0, 0.0, pltpu.roll(x, 1, axis=0))
    right = jnp.where(col == GRID_W - 1, 0.0, pltpu.roll(x, tr - 1, axis=0))
    hp = jnp.where(first, 0.0, hp_ref[...])
    hn = jnp.where(last, 0.0, hn_ref[...])
    up = jnp.concatenate([hp, x[:tr - GRID_W]], axis=0)
    down = jnp.concatenate([x[GRID_W:], hn], axis=0)
    d = dir_ref[...]
    shifted = jnp.where(d == 0, left, jnp.where(d == 1, right, jnp.where(d == 2, up, down)))
    o_ref[...] = x + mu_ref[...] * (shifted - x)


def _dir_codes(widths, n_dirs):
    codes = []
    for w in widths:
        q = w // n_dirs
        codes.append(np.repeat(np.arange(n_dirs, dtype=np.int32), q))
    return np.concatenate(codes)[None, :]


def _token_shift(p, mu_perm, col0, width, group_widths, n_groups, seq_len, grid_mode):
    t = p.shape[0]
    tr = 512
    cb0 = col0 // width
    n_dirs = 4 if grid_mode else 2
    dirs = jnp.asarray(_dir_codes(group_widths, n_dirs))
    main = pl.BlockSpec((tr, width), lambda i, g: (i, cb0 + g))
    mu_spec = pl.BlockSpec((1, width), lambda i, g: (0, cb0 + g))
    dir_spec = pl.BlockSpec((1, width), lambda i, g: (0, 0))
    out_spec = pl.BlockSpec((None, tr, width), lambda i, g: (g, i, 0))
    out_shape = jax.ShapeDtypeStruct((n_groups, t, width), F32)
    if not grid_mode:
        return pl.pallas_call(
            functools.partial(_shift_seq_kernel, seq_len),
            grid=(t // tr, n_groups),
            in_specs=[main, mu_spec, dir_spec],
            out_specs=out_spec,
            out_shape=out_shape,
            compiler_params=_cparams(("parallel", "parallel")),
            name="token_shift_seq",
        )(p, mu_perm, dirs)
    hb = tr // GRID_W
    n_hb = t // GRID_W
    halo_prev = pl.BlockSpec((GRID_W, width), lambda i, g: (jnp.maximum(i * hb - 1, 0), cb0 + g))
    halo_next = pl.BlockSpec((GRID_W, width), lambda i, g: (jnp.minimum((i + 1) * hb, n_hb - 1), cb0 + g))
    return pl.pallas_call(
        functools.partial(_shift_grid_kernel, seq_len),
        grid=(t // tr, n_groups),
        in_specs=[main, halo_prev, halo_next, mu_spec, dir_spec],
        out_specs=out_spec,
        out_shape=out_shape,
        compiler_params=_cparams(("parallel", "parallel")),
        name="token_shift_grid",
    )(p, p, p, mu_perm, dirs)


def _bd_rows(x):
    c = x.shape[0]
    xt = jnp.concatenate([x, x, x, x], axis=0)
    rb = lax.broadcasted_iota(jnp.int32, xt.shape, 0) // c
    lb = lax.broadcasted_iota(jnp.int32, xt.shape, 1) // HEAD_DIM
    return jnp.where(rb == lb, xt, jnp.zeros_like(xt))


def _dot_nt(a, b):
    return lax.dot_general(a, b, (((1,), (1,)), ((), ())), preferred_element_type=F32)


def _dot_tn(a, b):
    return lax.dot_general(a, b, (((0,), (0,)), ((), ())), preferred_element_type=F32)


def _wkv_chains(chains, ones_bd):
    c = chains[0]["r"].shape[0]
    ti = lax.broadcasted_iota(jnp.int32, (c, c), 0)
    si = lax.broadcasted_iota(jnp.int32, (c, c), 1)
    tri = {rev: jnp.where((si >= ti) if rev else (si <= ti), 1.0, 0.0).astype(BF16) for rev in (False, True)}
    t2 = lax.broadcasted_iota(jnp.int32, (c, 4 * c), 0)
    s2 = lax.broadcasted_iota(jnp.int32, (c, 4 * c), 1) % c
    strict = {False: s2 < t2, True: s2 > t2}
    incl = {False: s2 <= t2, True: s2 >= t2}
    eye = jnp.where(s2 == t2, 1.0, 0.0)

    for ch in chains:
        ch["kkr"] = ch["k"] * ch["kk_w"]
    for ch in chains:
        ch["ss"] = _dot_exact_rhs(ch["kkr"] * ch["kkr"], ones_bd)
    for ch in chains:
        ch["wlin"] = ch["w0"] + _dot3(jnp.tanh(ch["sm"][:, 0:64]), ch["wup"])
    for ch in chains:
        ch["a"] = _sigmoid(ch["a0"] + _dot3(ch["sm"][:, 64:128], ch["aup"]))
    for ch in chains:
        logw = -jnp.exp(-_softplus(-ch["wlin"]) - 0.5)
        lh = logw.astype(BF16)
        l1 = logw - lh.astype(F32)
        lm = l1.astype(BF16)
        ll = (l1 - lm.astype(F32)).astype(BF16)
        t = tri[ch["rev"]]
        ch["logw"] = logw
        ch["cum"] = _dot(t, lh) + (_dot(t, lm) + _dot(t, ll))
    for ch in chains:
        rev, cum, k = ch["rev"], ch["cum"], ch["k"]
        kk = ch["kkr"] * lax.rsqrt(jnp.maximum(ch["ss"], 1e-24))
        a = ch["a"]
        kd = k * (1.0 + (a - 1.0) * ch["ka_w"])
        b = kk * a
        ltot = cum[0:1, :] if rev else cum[c - 1:c, :]
        e_neg = jnp.exp(-cum)
        e_tail = jnp.exp(ltot - cum)
        at = -kk * jnp.exp(cum - ch["logw"])
        rt = ch["r"] * jnp.exp(cum)
        ch["ltot"] = ltot
        ch["ar"] = jnp.concatenate([at, rt], axis=0).astype(BF16)
        ch["bt"] = _bd_rows(b * e_neg).astype(BF16)
        ch["kt"] = _bd_rows(kd * e_neg).astype(BF16)
        ch["bk"] = jnp.concatenate([b * e_tail, kd * e_tail], axis=0).astype(BF16)
    for ch in chains:
        ch["m_b"] = _dot_nt(ch["ar"], ch["bt"])
    for ch in chains:
        ch["m_k"] = _dot_nt(ch["ar"], ch["kt"])
    for ch in chains:
        ch["st"] = _dot_nt(ch["ar"], ch["s0"].astype(BF16))
    for ch in chains:
        rev = ch["rev"]
        ch["m_ab"] = jnp.where(strict[rev], ch["m_b"][:c], 0.0)
        ch["n_rb"] = jnp.where(incl[rev], ch["m_b"][c:], 0.0).astype(BF16)
        ch["n_rk"] = jnp.where(incl[rev], ch["m_k"][c:], 0.0).astype(BF16)
        ch["v_bd"] = _bd_rows(ch["v"]).astype(BF16)
    for ch in chains:
        m_ak = jnp.where(strict[ch["rev"]], ch["m_k"][:c], 0.0).astype(BF16)
        ch["rhs"] = ch["st"][:c] + _dot(m_ak, ch["v_bd"])

    blk = 1
    while blk < c:
        tb = t2 // blk
        sb = s2 // blk
        off = {False: (tb % 2 == 1) & (sb == tb - 1), True: (tb % 2 == 0) & (sb == tb + 1)}
        if blk == 1:
            for ch in chains:
                ch["x"] = eye + jnp.where(off[ch["rev"]], ch["m_ab"], 0.0)
        else:
            for ch in chains:
                m_off = jnp.where(off[ch["rev"]], ch["m_ab"], 0.0)
                ch["p1"] = _dot(ch["x"].astype(BF16), _bd_rows(m_off).astype(BF16))
            for ch in chains:
                ch["x"] = ch["x"] + _dot(ch["p1"].astype(BF16), _bd_rows(ch["x"]).astype(BF16))
        blk *= 2

    for ch in chains:
        ch["u"] = _dot(ch["x"].astype(BF16), _bd_rows(ch["rhs"]).astype(BF16))
    for ch in chains:
        u_bd = _bd_rows(ch["u"]).astype(BF16)
        ch["o"] = ch["st"][c:] + _dot(ch["n_rb"], u_bd) + _dot(ch["n_rk"], ch["v_bd"])
    out = []
    for ch in chains:
        uv = jnp.concatenate([ch["u"], ch["v"]], axis=0).astype(BF16)
        upd = _dot_tn(uv, ch["bk"])
        out.append((ch["o"], ch["s0"] * jnp.exp(ch["ltot"]) + jnp.where(ones_bd > 0, upd, 0.0)))
    return out


def _wkv_kernel(zero_init, nc,
                rf_ref, kf_ref, vf_ref, smf_ref, rb_ref, kb_ref, vb_ref, smb_ref,
                kkw_ref, kaw_ref, w0_ref, wup_ref, a0_ref, aup_ref, s0_ref,
                of_ref, ob_ref, sout_ref, sf_ref, sb_ref):
    ci = pl.program_id(2)
    ones_bd = _head_ones(HG)
    n_g = sf_ref.shape[0]

    @pl.when(ci == 0)
    def _():
        sf_ref[...] = jnp.zeros_like(sf_ref)
        sb_ref[...] = jnp.zeros_like(sb_ref)
        if not zero_init:
            for d, ref in ((0, sf_ref), (1, sb_ref)):
                for h in range(4 * n_g):
                    sl = slice((h % 4) * HEAD_DIM, (h % 4 + 1) * HEAD_DIM)
                    ref[h // 4, sl, sl] = s0_ref[d, h]

    smf = smf_ref[...]
    smb = smb_ref[...]
    chains = []
    for g in range(n_g):
        ln = slice(g * HG, (g + 1) * HG)
        common = dict(kk_w=kkw_ref[:, ln], ka_w=kaw_ref[:, ln])
        chains.append(dict(rev=False, r=rf_ref[:, ln], k=kf_ref[:, ln], v=vf_ref[:, ln], sm=smf,
                           w0=w0_ref[0, :, ln], wup=wup_ref[0, :, ln], a0=a0_ref[0, :, ln],
                           aup=aup_ref[0, :, ln], s0=sf_ref[g], **common))
        chains.append(dict(rev=True, r=rb_ref[:, ln], k=kb_ref[:, ln], v=vb_ref[:, ln], sm=smb,
                           w0=w0_ref[1, :, ln], wup=wup_ref[1, :, ln], a0=a0_ref[1, :, ln],
                           aup=aup_ref[1, :, ln], s0=sb_ref[g], **common))
    results = _wkv_chains(chains, ones_bd)
    for g in range(n_g):
        ln = slice(g * HG, (g + 1) * HG)
        (o_f, s_f), (o_b, s_b) = results[2 * g], results[2 * g + 1]
        of_ref[:, ln] = o_f
        ob_ref[:, ln] = o_b
        sf_ref[g] = s_f
        sb_ref[g] = s_b

    @pl.when(ci == nc - 1)
    def _():
        for d, ref in ((0, sf_ref), (1, sb_ref)):
            for h in range(4 * n_g):
                sl = slice((h % 4) * HEAD_DIM, (h % 4 + 1) * HEAD_DIM)
                sout_ref[d, h] = ref[h // 4, sl, sl]


def _wkv_scan(rkv, small, n_seq, seq_len, s0, wts):
    t = rkv.shape[1]
    c = WKV_CHUNK
    nc = seq_len // c
    gs = WKV_GROUPS_PER_STEP
    wl = gs * HG
    ng = D_MODEL // wl
    zero_init = s0 is None
    if zero_init:
        s0 = jnp.zeros((1, 2, N_HEADS, HEAD_DIM, HEAD_DIM), F32)

    def fwd(which):
        return pl.BlockSpec((None, c, wl), lambda b, g, ci: (which, b * nc + ci, g))

    def bwd(which):
        return pl.BlockSpec((None, c, wl), lambda b, g, ci: (which, b * nc + (nc - 1 - ci), g))

    sm_f = pl.BlockSpec((None, c, 256), lambda b, g, ci: (0, b * nc + ci, 0))
    sm_b = pl.BlockSpec((None, c, 256), lambda b, g, ci: (0, b * nc + (nc - 1 - ci), 0))
    vec = pl.BlockSpec((1, wl), lambda b, g, ci: (0, g))
    vec2 = pl.BlockSpec((2, 1, wl), lambda b, g, ci: (0, 0, g))
    up2 = pl.BlockSpec((2, 64, wl), lambda b, g, ci: (0, 0, g))
    if zero_init:
        s0_spec = pl.BlockSpec((None, 2, 4 * gs, HEAD_DIM, HEAD_DIM), lambda b, g, ci: (0, 0, g, 0, 0))
    else:
        s0_spec = pl.BlockSpec((None, 2, 4 * gs, HEAD_DIM, HEAD_DIM), lambda b, g, ci: (b, 0, g, 0, 0))
    o_f = pl.BlockSpec((c, wl), lambda b, g, ci: (b * nc + ci, g))
    o_b = pl.BlockSpec((c, wl), lambda b, g, ci: (b * nc + (nc - 1 - ci), g))
    s_out = pl.BlockSpec((None, 2, 4 * gs, HEAD_DIM, HEAD_DIM), lambda b, g, ci: (b, 0, g, 0, 0))
    return pl.pallas_call(
        functools.partial(_wkv_kernel, zero_init, nc),
        grid=(n_seq, ng, nc),
        in_specs=[fwd(0), fwd(1), fwd(2), sm_f, bwd(0), bwd(1), bwd(2), sm_b,
                  vec, vec, vec2, up2, vec2, up2, s0_spec],
        out_specs=[o_f, o_b, s_out],
        out_shape=[jax.ShapeDtypeStruct((t, D_MODEL), F32), jax.ShapeDtypeStruct((t, D_MODEL), F32),
                   jax.ShapeDtypeStruct((n_seq, 2, N_HEADS, HEAD_DIM, HEAD_DIM), F32)],
        scratch_shapes=[pltpu.VMEM((gs, HG, HG), F32), pltpu.VMEM((gs, HG, HG), F32)],
        compiler_params=_cparams(("parallel", "parallel", "arbitrary")),
        name="wkv_scan",
    )(rkv, rkv, rkv, small, rkv, rkv, rkv, small,
      wts["k_k"], wts["k_a"], wts["w0"], wts["w_up"], wts["a0"], wts["a_up"], s0)


LRU_CH = 256
LRU_TILE = 384


def _gelu_tanh(x):
    return 0.5 * x * (1.0 + jnp.tanh(0.7978845608028654 * (x + 0.044715 * (x * x * x))))


def _lru_kernel(seq_len, n_rows, lx_ref, lg_ref, cw_ref, cb_ref, wg_ref, bg_ref, lam_ref, h0_ref,
                y_ref, hfin_ref):
    n_ch = n_rows // LRU_CH
    hfin_ref[...] = jnp.zeros_like(hfin_ref)
    cw = cw_ref[...]
    cb = cb_ref[...]

    def chunk(d, ci, hc):
        rev = d == 1
        start = pl.multiple_of((n_ch - 1 - ci if rev else ci) * LRU_CH, LRU_CH)
        at_seq_start = (start % seq_len) == 0
        at_seq_end = ((start + LRU_CH) % seq_len) == 0
        prev8 = lx_ref[pl.ds(pl.multiple_of(jnp.maximum(start - 8, 0), 8), 8), :]
        nxt8 = lx_ref[pl.ds(pl.multiple_of(jnp.minimum(start + LRU_CH, n_rows - 8), 8), 8), :]
        prev8 = jnp.where(at_seq_start, 0.0, prev8)
        nxt8 = jnp.where(at_seq_end, 0.0, nxt8)
        cur = lx_ref[pl.ds(start, LRU_CH), :]
        ext = jnp.concatenate([prev8, cur, nxt8], axis=0)
        n_ext = LRU_CH + 16
        xm2 = pltpu.roll(ext, 2, axis=0)[8:8 + LRU_CH]
        xm1 = pltpu.roll(ext, 1, axis=0)[8:8 + LRU_CH]
        xp1 = pltpu.roll(ext, n_ext - 1, axis=0)[8:8 + LRU_CH]
        xb = cb + xm2 * cw[0:1] + xm1 * cw[1:2] + cur * cw[2:3] + xp1 * cw[3:4]
        pre = _dot(xb.astype(BF16), wg_ref[d]) + bg_ref[d]
        r_g = _sigmoid(pre[:, :LRU_TILE])
        i_g = _sigmoid(pre[:, LRU_TILE:])
        log_a = (LRU_C * r_g) * (-_softplus(-lam_ref[d]))
        a_val = jnp.exp(log_a)
        th = jnp.tanh(log_a)
        u_val = jnp.sqrt(-2.0 * th / (1.0 - th)) * i_g * xb

        sub = lax.broadcasted_iota(jnp.int32, (LRU_CH, LRU_TILE), 0) % 8
        for s in (1, 2, 4):
            shift = LRU_CH - s if rev else s
            valid = (sub < 8 - s) if rev else (sub >= s)
            a_sh = pltpu.roll(a_val, shift, axis=0)
            u_sh = pltpu.roll(u_val, shift, axis=0)
            u_val = jnp.where(valid, a_val * u_sh + u_val, u_val)
            a_val = jnp.where(valid, a_val * a_sh, a_val)
        hc = jnp.where(at_seq_end if rev else at_seq_start, h0_ref[d], hc)
        n_grp = LRU_CH // 8
        pieces = [None] * n_grp
        for j in (range(n_grp - 1, -1, -1) if rev else range(n_grp)):
            h = u_val[8 * j:8 * j + 8] + a_val[8 * j:8 * j + 8] * hc
            pieces[j] = h
            hc = h[0:1] if rev else h[7:8]
        h_all = jnp.concatenate(pieces, axis=0)
        rows = pl.ds(start, LRU_CH)
        if rev:
            y_ref[rows, :] = (y_ref[rows, :] + h_all) * _gelu_tanh(lg_ref[rows, :])
        else:
            y_ref[rows, :] = h_all
        seq_i = start // seq_len

        @pl.when(at_seq_start if rev else at_seq_end)
        def _():
            hfin_ref[d, pl.ds(seq_i, 1), :] = hc

        return hc

    for d in range(2):
        lax.fori_loop(0, n_ch, functools.partial(chunk, d), jnp.zeros((1, LRU_TILE), F32))


def _lru_scan(p, n_seq, seq_len, h0, wts):
    t = p.shape[0]
    n_rows = max(seq_len, 8 * LRU_CH)
    nb = t // n_rows
    nt = D_RNN // LRU_TILE
    chain = h0 is not None
    if not chain:
        h0 = jnp.zeros((1, 2, 1, D_RNN), F32)
        h0_spec = pl.BlockSpec((None, 2, 1, LRU_TILE), lambda i, j: (0, 0, 0, j))
    else:
        h0 = h0.reshape(n_seq, 2, 1, D_RNN)
        h0_spec = pl.BlockSpec((None, 2, 1, LRU_TILE), lambda i, j: (i, 0, 0, j))
    lg0 = COL_LG // LRU_TILE
    y, hfin = pl.pallas_call(
        functools.partial(_lru_kernel, seq_len, n_rows),
        grid=(nb, nt),
        in_specs=[
            pl.BlockSpec((n_rows, LRU_TILE), lambda i, j: (i, j)),
            pl.BlockSpec((n_rows, LRU_TILE), lambda i, j: (i, lg0 + j)),
            pl.BlockSpec((CONV_W, LRU_TILE), lambda i, j: (0, j)),
            pl.BlockSpec((1, LRU_TILE), lambda i, j: (0, j)),
            pl.BlockSpec((2, None, LRU_TILE, 2 * LRU_TILE), lambda i, j: (0, j, 0, 0)),
            pl.BlockSpec((2, None, 1, 2 * LRU_TILE), lambda i, j: (0, j, 0, 0)),
            pl.BlockSpec((2, 1, LRU_TILE), lambda i, j: (0, 0, j)),
            h0_spec,
        ],
        out_specs=[
            pl.BlockSpec((n_rows, LRU_TILE), lambda i, j: (i, j)),
            pl.BlockSpec((2, None, 8, LRU_TILE), lambda i, j: (0, i, 0, j)),
        ],
        out_shape=[jax.ShapeDtypeStruct((t, D_RNN), F32), jax.ShapeDtypeStruct((2, nb, 8, D_RNN), F32)],
        compiler_params=_cparams(("parallel", "parallel")),
        name="lru_scan",
    )(p, p, wts["conv_w"], wts["conv_b"], wts["lru_wg"], wts["lru_bg"], wts["lru_lam"], h0)
    return y, hfin


def _lru_gate_weights(wa, ba, wx, bx):
    nt = D_RNN // LRU_TILE
    per = LRU_TILE // LRU_BLOCK
    eye = jnp.eye(per, dtype=F32)

    def tiles(w):
        w = w.reshape(2, nt, per, LRU_BLOCK, LRU_BLOCK)
        bd = jnp.einsum("dtaij,ab->dtaibj", w, eye)
        return bd.reshape(2, nt, LRU_TILE, LRU_TILE)

    wg = jnp.concatenate([tiles(wa), tiles(wx)], axis=-1).astype(BF16)
    bg = jnp.concatenate([ba.reshape(2, nt, 1, LRU_TILE), bx.reshape(2, nt, 1, LRU_TILE)], axis=-1)
    return wg, bg


ROUTER_LANES = 128
ROUTE_TILE = 256
PACK_W = 256


def _pack_bf16_pairs(x):
    bits = pltpu.bitcast(x.astype(BF16).astype(F32), jnp.uint32)
    hi_mask = jnp.uint32(0xFFFF0000)

    def pack(hi, lo):
        return (hi & hi_mask) | (lo >> 16)

    return (pack(bits[:, 0:PACK_W], bits[:, PACK_W:2 * PACK_W]),
            pack(bits[:, 2 * PACK_W:3 * PACK_W], bits[:, 3 * PACK_W:4 * PACK_W]))


def _unpack_bf16_pairs(pa, pb):
    hi_mask = jnp.uint32(0xFFFF0000)
    parts = []
    for p in (pa, pb):
        parts.append(pltpu.bitcast(p & hi_mask, F32))
        parts.append(pltpu.bitcast(p << 16, F32))
    return jnp.concatenate(parts, axis=1)


def _head_sum(x, ones_bd):
    parts = [_dot_exact_rhs(x[:, g * HG:(g + 1) * HG], ones_bd) for g in range(x.shape[1] // HG)]
    return jnp.concatenate(parts, axis=1)


def _merge_kernel(o0_ref, o1_ref, rkv_ref, sm_ref, ga_ref, gb_ref, yb_ref, x_ref, mod_ref,
                  aup_ref, a0_ref, ka_ref, rk_ref, gup_ref, lng_ref, lnb_ref, proj_ref, bm_ref,
                  wout_ref, g2_ref, rw_ref, rb_ref, x1_ref, ha_ref, hb_ref, route_ref, cnt_ref):
    ones_bd = _head_ones(HG)
    o = o0_ref[...] + o1_ref[...]
    mu = _head_sum(o, ones_bd) * (1.0 / HEAD_DIM)
    oc = o - mu
    var = _head_sum(oc * oc, ones_bd) * (1.0 / HEAD_DIM)
    on = oc * lax.rsqrt(var + GN_EPS)
    r = rkv_ref[0]
    k = rkv_ref[1]
    v = rkv_ref[2]
    sm = sm_ref[...]
    ad = sm[:, 64:128]
    gd = sm[:, 128:256]
    ka = ka_ref[...]
    kd_rk = jnp.zeros_like(k)
    for d in range(2):
        a = _sigmoid(a0_ref[d] + _dot3(ad, aup_ref[d]))
        kd_rk = kd_rk + k * (1.0 + (a - 1.0) * ka) * rk_ref[d]
    bonus = _head_sum(r * kd_rk, ones_bd) * v
    g = _dot(_sigmoid(gd).astype(BF16), gup_ref[...])
    y_a = (on * lng_ref[...] + lnb_ref[...] + bonus) * g
    y_b = _dot(yb_ref[...].astype(BF16), proj_ref[...])
    bm = bm_ref[...]
    gate_a = _sigmoid(ga_ref[...] + bm[:, :D_MODEL])
    gate_b = _sigmoid(gb_ref[...] + bm[:, D_MODEL:])
    y = gate_a * y_a + gate_b * y_b
    m = mod_ref[...]
    gt1 = m[:, 2 * D_MODEL:3 * D_MODEL]
    sh2 = m[:, 3 * D_MODEL:4 * D_MODEL]
    sc2 = m[:, 4 * D_MODEL:5 * D_MODEL]
    x1 = x_ref[...] + gt1 * _dot(y.astype(BF16), wout_ref[...])
    x1_ref[...] = x1
    ms = jnp.mean(x1 * x1, axis=-1, keepdims=True)
    h2 = (x1 * lax.rsqrt(ms + RMS_EPS) * g2_ref[...]) * (1.0 + sc2) + sh2
    ha_ref[...], hb_ref[...] = _pack_bf16_pairs(h2)

    logits = _dot3(h2, rw_ref[...]) + rb_ref[...]
    lane = lax.broadcasted_iota(jnp.int32, logits.shape, 1)
    work = logits
    vals, sels, firsts = [], [], []
    for _ in range(TOP_K):
        mx = jnp.max(work, axis=-1, keepdims=True)
        first = jnp.min(jnp.where(work == mx, lane, ROUTER_LANES), axis=-1, keepdims=True)
        sel = lane == first
        vals.append(mx)
        sels.append(sel)
        firsts.append(first)
        work = jnp.where(sel, -jnp.inf, work)
    es = [jnp.exp(val - vals[0]) for val in vals]
    inv = 1.0 / (es[0] + es[1] + es[2] + es[3])
    mask = jnp.where(sels[0] | sels[1] | sels[2] | sels[3], 1.0, 0.0)
    tm = mask.shape[0]
    ri = lax.broadcasted_iota(jnp.int32, (tm, tm), 0)
    ci = lax.broadcasted_iota(jnp.int32, (tm, tm), 1)
    earlier = jnp.where(ci < ri, 1.0, 0.0).astype(BF16)
    rank = _dot(earlier, mask.astype(BF16))
    route = jnp.zeros_like(logits)
    for j in range(TOP_K):
        rank_j = jnp.sum(jnp.where(sels[j], rank, 0.0), axis=-1, keepdims=True)
        route = jnp.where(lane == j, firsts[j].astype(F32), route)
        route = jnp.where(lane == TOP_K + j, es[j] * inv, route)
        route = jnp.where(lane == 2 * TOP_K + j, rank_j, route)
    route_ref[...] = route
    cnt_ref[...] = jnp.broadcast_to(jnp.sum(mask, axis=0, keepdims=True), cnt_ref.shape)


def _merge(o0, o1, rkv, small, p, yb, x, mod3, mod_base, rows_per_mod, w):
    t = x.shape[0]
    tm = ROUTE_TILE

    def row(width, col=0):
        return pl.BlockSpec((tm, width), lambda i: (i, col))

    def full(shape):
        nd = len(shape)
        return pl.BlockSpec(shape, lambda i: (0,) * nd)

    return pl.pallas_call(
        _merge_kernel,
        grid=(t // tm,),
        in_specs=[
            row(D_MODEL), row(D_MODEL),
            pl.BlockSpec((3, tm, D_MODEL), lambda i: (0, i, 0)),
            pl.BlockSpec((None, tm, 256), lambda i: (0, i, 0)),
            row(D_MODEL, COL_GA // D_MODEL), row(D_MODEL, COL_GB // D_MODEL),
            row(D_RNN), row(D_MODEL),
            pl.BlockSpec((None, 1, 6 * D_MODEL), lambda i: (mod_base + (i * tm) // rows_per_mod, 0, 0)),
            full((2, 64, D_MODEL)), full((2, 1, D_MODEL)), full((1, D_MODEL)), full((2, 1, D_MODEL)),
            full((128, D_MODEL)), full((1, D_MODEL)), full((1, D_MODEL)), full((D_RNN, D_MODEL)),
            full((1, 2 * D_MODEL)), full((D_MODEL, D_MODEL)), full((1, D_MODEL)),
            full((D_MODEL, ROUTER_LANES)), full((1, ROUTER_LANES)),
        ],
        out_specs=[row(D_MODEL), row(PACK_W), row(PACK_W), row(ROUTER_LANES),
                   pl.BlockSpec((None, 8, ROUTER_LANES), lambda i: (i, 0, 0))],
        out_shape=[jax.ShapeDtypeStruct((t, D_MODEL), F32),
                   jax.ShapeDtypeStruct((t, PACK_W), jnp.uint32), jax.ShapeDtypeStruct((t, PACK_W), jnp.uint32),
                   jax.ShapeDtypeStruct((t, ROUTER_LANES), F32),
                   jax.ShapeDtypeStruct((t // tm, 8, ROUTER_LANES), F32)],
        compiler_params=_cparams(("parallel",)),
        name="merge",
    )(o0, o1, rkv, small, p, p, yb, x, mod3,
      w["a_up"], w["a0"], w["k_a"], w["r_k"], w["g_up"], w["ln_g"], w["ln_b"], w["lru_proj"],
      w["b_merge"], w["w_out"], w["norm_ffn_g"], w["router_w"], w["router_b"])


EXPERT_TILE = 512
SC_WINDOW = 128


def _sc_mesh():
    return plsc.VectorSubcoreMesh(core_axis_name="core", subcore_axis_name="subcore")


def _sc_scatter_rows(x, pos_flat, n_out):
    t, width = x.shape
    n_idx = pos_flat.shape[1]
    n_src = t // SC_WINDOW

    @pl.kernel(out_type=jax.ShapeDtypeStruct((n_out, width), x.dtype), mesh=_sc_mesh(), scratch_types=[])
    def scatter(x_hbm, i_hbm, o_hbm):
        def body(x_vmem, i_vmem):
            pltpu.sync_copy(x_vmem, o_hbm.at[i_vmem.at[0]])

        pltpu.emit_pipeline(
            body,
            grid=(n_idx // SC_WINDOW,),
            in_specs=[pl.BlockSpec((SC_WINDOW, width), index_map=lambda i: (i % n_src, 0)),
                      pl.BlockSpec((1, SC_WINDOW), index_map=lambda i: (0, i))],
            out_specs=[],
            core_axis_name=("core", "subcore"),
            dimension_semantics=(pltpu.PARALLEL,),
        )(x_hbm, i_hbm)

    return scatter(x, pos_flat)


def _sc_gather_rows(y, pos_flat):
    n_idx = pos_flat.shape[1]
    width = y.shape[1]

    @pl.kernel(out_type=jax.ShapeDtypeStruct((n_idx, width), y.dtype), mesh=_sc_mesh(), scratch_types=[])
    def gather(y_hbm, i_hbm, o_hbm):
        def body(i_vmem, o_vmem):
            pltpu.sync_copy(y_hbm.at[i_vmem.at[0]], o_vmem)

        pltpu.emit_pipeline(
            body,
            grid=(n_idx // SC_WINDOW,),
            in_specs=[pl.BlockSpec((1, SC_WINDOW), index_map=lambda i: (0, i))],
            out_specs=[pl.BlockSpec((SC_WINDOW, width), index_map=lambda i: (i, 0))],
            core_axis_name=("core", "subcore"),
            dimension_semantics=(pltpu.PARALLEL,),
        )(i_hbm, o_hbm)

    return gather(y, pos_flat)


def _expert_kernel(te_ref, nt_ref, xa_ref, xb_ref, wgu_ref, bgu_ref, wd_ref, bd_ref, ya_ref, yb_ref,
                   wgu_bf, wd_bf):
    i = pl.program_id(0)
    prev = te_ref[jnp.maximum(i - 1, 0)]
    fresh = (i == 0) | (te_ref[i] != prev)

    @pl.when(fresh)
    def _():
        wgu_bf[...] = wgu_ref[...].astype(BF16)
        wd_bf[...] = wd_ref[...].astype(BF16)

    @pl.when(i < nt_ref[0])
    def _():
        d_exp = wd_bf.shape[0]
        bgu = bgu_ref[...]
        half = EXPERT_TILE // 2
        for s in range(2):
            rows = pl.ds(s * half, half)
            x = _unpack_bf16_pairs(xa_ref[rows, :], xb_ref[rows, :]).astype(BF16)
            gate = _dot(x, wgu_bf[:, :d_exp]) + bgu[:, :d_exp]
            up = _dot(x, wgu_bf[:, d_exp:]) + bgu[:, d_exp:]
            gate = jnp.minimum(gate, SWIGLU_LIMIT)
            up = jnp.clip(up, -SWIGLU_LIMIT, SWIGLU_LIMIT)
            act = gate * _sigmoid(SWIGLU_ALPHA * gate) * (up + 1.0)
            y = _dot(act.astype(BF16), wd_bf[...]) + bd_ref[...]
            ya_ref[rows, :], yb_ref[rows, :] = _pack_bf16_pairs(y)


def _experts(xs_a, xs_b, tile_expert, n_tiles_used, w):
    n_rows = xs_a.shape[0]
    n_tiles = n_rows // EXPERT_TILE
    n_exp, _, two_d = w["exp_w_gu"].shape
    d_exp = two_d // 2

    def rows(i, te, nt):
        return (jnp.minimum(i, nt[0] - 1), 0)

    def by_expert(i, te, nt):
        return (te[i], 0, 0)

    grid_spec = pltpu.PrefetchScalarGridSpec(
        num_scalar_prefetch=2,
        grid=(n_tiles,),
        in_specs=[
            pl.BlockSpec((EXPERT_TILE, PACK_W), rows),
            pl.BlockSpec((EXPERT_TILE, PACK_W), rows),
            pl.BlockSpec((None, D_MODEL, two_d), by_expert),
            pl.BlockSpec((None, 1, two_d), by_expert),
            pl.BlockSpec((None, d_exp, D_MODEL), by_expert),
            pl.BlockSpec((None, 1, D_MODEL), by_expert),
        ],
        out_specs=[pl.BlockSpec((EXPERT_TILE, PACK_W), rows), pl.BlockSpec((EXPERT_TILE, PACK_W), rows)],
        scratch_shapes=[pltpu.VMEM((D_MODEL, two_d), BF16), pltpu.VMEM((d_exp, D_MODEL), BF16)],
    )
    return pl.pallas_call(
        _expert_kernel,
        grid_spec=grid_spec,
        out_shape=[jax.ShapeDtypeStruct((n_rows, PACK_W), jnp.uint32)] * 2,
        compiler_params=_cparams(("arbitrary",)),
        name="experts",
    )(tile_expert, n_tiles_used, xs_a, xs_b, w["exp_w_gu"], w["exp_b_gu"], w["exp_w_down"], w["exp_b_down"])


def _combine_kernel(ya_ref, yb_ref, route_ref, x1_ref, mod_ref, fg_ref, o_ref):
    route = route_ref[...]
    lane = lax.broadcasted_iota(jnp.int32, route.shape, 1)
    acc = jnp.zeros(x1_ref.shape, F32)
    for j in range(TOP_K):
        w_j = jnp.sum(jnp.where(lane == TOP_K + j, route, 0.0), axis=-1, keepdims=True)
        acc = acc + w_j * _unpack_bf16_pairs(ya_ref[j], yb_ref[j])
    gt2 = mod_ref[...][:, 5 * D_MODEL:6 * D_MODEL]
    x2 = x1_ref[...] + gt2 * acc
    ms = jnp.mean(x2 * x2, axis=-1, keepdims=True)
    o_ref[...] = x2 * lax.rsqrt(ms + RMS_EPS) * fg_ref[...]


def _combine(yg_a, yg_b, route, x1, mod3, mod_base, rows_per_mod, row0, w):
    t = x1.shape[0]
    tm = ROUTE_TILE
    b0 = row0 // tm
    return pl.pallas_call(
        _combine_kernel,
        grid=(t // tm,),
        in_specs=[
            pl.BlockSpec((TOP_K, tm, PACK_W), lambda i: (0, b0 + i, 0)),
            pl.BlockSpec((TOP_K, tm, PACK_W), lambda i: (0, b0 + i, 0)),
            pl.BlockSpec((tm, ROUTER_LANES), lambda i: (b0 + i, 0)),
            pl.BlockSpec((tm, D_MODEL), lambda i: (i, 0)),
            pl.BlockSpec((None, 1, 6 * D_MODEL), lambda i: (mod_base + (i * tm) // rows_per_mod, 0, 0)),
            pl.BlockSpec((1, D_MODEL), lambda i: (0, 0)),
        ],
        out_specs=pl.BlockSpec((tm, D_MODEL), lambda i: (i, 0)),
        out_shape=jax.ShapeDtypeStruct((t, D_MODEL), F32),
        compiler_params=_cparams(("parallel",)),
        name="combine",
    )(yg_a, yg_b, route, x1, mod3, w["final_norm_g"])


def _routing_tables(route, cnt, n_exp):
    t = route.shape[0]
    e4 = route[:, 0:TOP_K].astype(jnp.int32)
    r4 = route[:, 2 * TOP_K:3 * TOP_K].astype(jnp.int32)
    cnt = cnt[:, 0, :n_exp].astype(jnp.int32)
    tile_off = jnp.cumsum(cnt, axis=0) - cnt
    n_e = jnp.sum(cnt, axis=0)
    region = (n_e + EXPERT_TILE - 1) // EXPERT_TILE * EXPERT_TILE
    region_end = jnp.cumsum(region)
    base = region_end - region
    tile_of_token = jnp.arange(t, dtype=jnp.int32) // ROUTE_TILE
    pos4 = base[e4] + tile_off[tile_of_token[:, None], e4] + r4
    n_tiles_max = (TOP_K * t + n_exp * (EXPERT_TILE - 1)) // EXPERT_TILE
    tile_start = jnp.arange(n_tiles_max, dtype=jnp.int32) * EXPERT_TILE
    tile_expert = jnp.minimum(jnp.searchsorted(region_end, tile_start, side="right"), n_exp - 1)
    n_used = (region_end[-1] // EXPERT_TILE).reshape(1)
    return pos4.T.reshape(1, TOP_K * t), tile_expert.astype(jnp.int32), n_used.astype(jnp.int32), n_tiles_max


_N_SHIFT = 3328
_PERM = np.concatenate([
    np.arange(_N_SHIFT, _N_SHIFT + 2 * D_RNN),
    np.arange(0, 3 * D_MODEL),
    np.arange(_N_SHIFT + 2 * D_RNN, _N_SHIFT + 2 * D_RNN + 2 * D_MODEL),
    np.arange(3 * D_MODEL, _N_SHIFT),
])


def _stream(x, n_seq, seq_len, mod3, mod_base, rows_per_mod, s_wkv0, s_lru0, grid_mode, w):
    t = n_seq * seq_len
    x2 = x.reshape(t, D_MODEL)
    p = _in_projection(x2, mod3, mod_base, rows_per_mod, w["norm_mix_g"], w["w_in"])
    rkv = _token_shift(p, w["mu"], COL_R, D_MODEL, (D_MODEL,), 3, seq_len, grid_mode)
    small = _token_shift(p, w["mu"], COL_SMALL, 256, (64, 64, 128), 1, seq_len, grid_mode)
    o0, o1, s_wkv = _wkv_scan(rkv, small, n_seq, seq_len, s_wkv0, w)
    yb, s_lru = _lru_scan(p, n_seq, seq_len, s_lru0, w)
    x1, h_a, h_b, route, cnt = _merge(o0, o1, rkv, small, p, yb, x2, mod3, mod_base, rows_per_mod, w)
    return dict(x1=x1, h_a=h_a, h_b=h_b, route=route, cnt=cnt, s_wkv=s_wkv, s_lru=s_lru)


def _moe_and_norm(streams, mods, n_exp, w):
    h_a = jnp.concatenate([s["h_a"] for s in streams], axis=0)
    h_b = jnp.concatenate([s["h_b"] for s in streams], axis=0)
    route = jnp.concatenate([s["route"] for s in streams], axis=0)
    cnt = jnp.concatenate([s["cnt"] for s in streams], axis=0)
    t_all = route.shape[0]
    pos_flat, tile_expert, n_used, n_tiles_max = _routing_tables(route, cnt, n_exp)
    n_rows = n_tiles_max * EXPERT_TILE
    xs_a = _sc_scatter_rows(h_a, pos_flat, n_rows)
    xs_b = _sc_scatter_rows(h_b, pos_flat, n_rows)
    ys_a, ys_b = _experts(xs_a, xs_b, tile_expert, n_used, w)
    yg_a = _sc_gather_rows(ys_a, pos_flat).reshape(TOP_K, t_all, PACK_W)
    yg_b = _sc_gather_rows(ys_b, pos_flat).reshape(TOP_K, t_all, PACK_W)
    outs, row0 = [], 0
    for s, (mod3, mod_base, rows_per_mod) in zip(streams, mods):
        outs.append(_combine(yg_a, yg_b, route, s["x1"], mod3, mod_base, rows_per_mod, row0, w))
        row0 += s["x1"].shape[0]
    return outs


def kernel(x_prompt, x_sample, state_wkv, state_lru, c, c_ctx, norm_mix_g, norm_ffn_g, w_ada, b_ada, w_in, shift_mu, wkv_k_k, wkv_k_a, wkv_r_k, wkv_w0, wkv_w_up, wkv_a0, wkv_a_up, wkv_g_up, wkv_ln_g, wkv_ln_b, lru_conv_w, lru_conv_b, lru_wa, lru_ba, lru_wx, lru_bx, lru_lambda, lru_proj, b_merge, w_out, router_w, router_b, exp_w_gu, exp_b_gu, exp_w_down, exp_b_down, final_norm_g):
    n_ctx, seq, _ = x_prompt.shape
    n_dec, dec_seq, _ = x_sample.shape
    assert w_in.shape[0] == 1, "single trunk layer: the final norm is fused into the MoE kernel"
    assert 1 + n_dec <= 8 and dec_seq % GRID_W == 0
    l = 0
    n_exp = router_w.shape[-1]
    cond8 = jnp.concatenate([c_ctx[None, :], c, jnp.zeros((8 - 1 - n_dec, D_MODEL), F32)], axis=0)
    mod3 = _modulation(cond8, w_ada[l], b_ada[l]).reshape(8, 1, 6 * D_MODEL)
    mu_full = jnp.concatenate([shift_mu[l], jnp.zeros((N_IN - _N_SHIFT,), F32)])
    lru_wg, lru_bg = _lru_gate_weights(lru_wa[l], lru_ba[l], lru_wx[l], lru_bx[l])
    router_w_pad = jnp.zeros((D_MODEL, ROUTER_LANES), F32).at[:, :n_exp].set(router_w[l])
    router_b_pad = jnp.full((1, ROUTER_LANES), -1e30, F32).at[0, :n_exp].set(router_b[l])
    w = dict(
        norm_mix_g=norm_mix_g[l],
        w_in=w_in[l][:, _PERM].astype(BF16),
        mu=mu_full[_PERM].reshape(1, N_IN),
        k_k=wkv_k_k[l].reshape(1, D_MODEL), k_a=wkv_k_a[l].reshape(1, D_MODEL),
        w0=wkv_w0[l].reshape(2, 1, D_MODEL), w_up=wkv_w_up[l],
        a0=wkv_a0[l].reshape(2, 1, D_MODEL), a_up=wkv_a_up[l],
        r_k=wkv_r_k[l].reshape(2, 1, D_MODEL), g_up=wkv_g_up[l].astype(BF16),
        ln_g=wkv_ln_g[l].reshape(1, D_MODEL), ln_b=wkv_ln_b[l].reshape(1, D_MODEL),
        conv_w=lru_conv_w[l], conv_b=lru_conv_b[l].reshape(1, D_RNN),
        lru_wg=lru_wg, lru_bg=lru_bg, lru_lam=lru_lambda[l].reshape(2, 1, D_RNN),
        lru_proj=lru_proj[l].astype(BF16), b_merge=b_merge[l].reshape(1, 2 * D_MODEL),
        w_out=w_out[l].astype(BF16), norm_ffn_g=norm_ffn_g[l].reshape(1, D_MODEL),
        router_w=router_w_pad, router_b=router_b_pad,
        exp_w_gu=exp_w_gu[l], exp_b_gu=exp_b_gu[l].reshape(n_exp, 1, -1),
        exp_w_down=exp_w_down[l], exp_b_down=exp_b_down[l].reshape(n_exp, 1, D_MODEL),
        final_norm_g=final_norm_g.reshape(1, D_MODEL),
    )
    ctx = _stream(x_prompt, n_ctx, seq, mod3, 0, n_ctx * seq, None, None, False, w)
    smp = _stream(x_sample, n_dec, dec_seq, mod3, 1, dec_seq, state_wkv[:, l], state_lru[:, l], True, w)
    y_prompt, y_sample = _moe_and_norm([ctx, smp], [(mod3, 0, n_ctx * seq), (mod3, 1, dec_seq)], n_exp, w)
    new_lru = ctx["s_lru"].reshape(2, n_ctx, D_RNN).transpose(1, 0, 2)
    return (y_prompt.reshape(x_prompt.shape), y_sample.reshape(x_sample.shape),
            ctx["s_wkv"][:, None].astype(state_wkv.dtype), new_lru[:, None].astype(state_lru.dtype))
```

```python
import functools

import numpy as np
import jax
import jax.numpy as jnp
from jax import lax
from jax.experimental import pallas as pl
from jax.experimental.pallas import tpu as pltpu
from jax.experimental.pallas import tpu_sc as plsc

F32 = jnp.float32
BF16 = jnp.bfloat16

D_MODEL = 1024
HEAD_DIM = 64
N_HEADS = 16
GRID_W = 64
D_RNN = 1536
LRU_BLOCK = 96
CONV_W = 4
LRU_C = 8.0
TOP_K = 4
SWIGLU_LIMIT = 7.0
SWIGLU_ALPHA = 1.702
RMS_EPS = 1e-6
GN_EPS = 64e-5

COL_LX = 0
COL_LG = 1536
COL_R = 3072
COL_K = 4096
COL_V = 5120
COL_GA = 6144
COL_GB = 7168
COL_SMALL = 8192
N_IN = 8448

MXU_DEPTH = 256
HG = 256
WKV_CHUNK = 64
WKV_GROUPS_PER_STEP = 4
VMEM_LIMIT = 56 * 1024 * 1024


def _cparams(sem, vmem=VMEM_LIMIT):
    return pltpu.CompilerParams(dimension_semantics=sem, vmem_limit_bytes=vmem)


def _split2(x):
    hi = x.astype(BF16)
    lo = (x - hi.astype(F32)).astype(BF16)
    return hi, lo


def _dot(a, b):
    return jnp.dot(a, b, preferred_element_type=F32)


def _dot3(a, b):
    ah, al = _split2(a)
    bh, bl = _split2(b)
    if 3 * a.shape[1] <= MXU_DEPTH:
        return _dot(jnp.concatenate([ah, al, ah], axis=1), jnp.concatenate([bh, bh, bl], axis=0))
    return _dot(ah, bh) + (_dot(al, bh) + _dot(ah, bl))


def _dot_exact_rhs(a, b_bf16):
    ah, al = _split2(a)
    return _dot(ah, b_bf16) + _dot(al, b_bf16)


def _head_ones(n):
    r = lax.broadcasted_iota(jnp.int32, (n, n), 0) // HEAD_DIM
    c = lax.broadcasted_iota(jnp.int32, (n, n), 1) // HEAD_DIM
    return jnp.where(r == c, 1.0, 0.0).astype(BF16)


def _sigmoid(x):
    return 0.5 * jnp.tanh(0.5 * x) + 0.5


def _softplus(x):
    return jnp.maximum(x, 0.0) + jnp.log(1.0 + jnp.exp(-jnp.abs(x)))


def _mod_kernel(c_ref, w_ref, b_ref, o_ref):
    c = c_ref[...]
    s = c * _sigmoid(c)
    o_ref[...] = _dot3(s, w_ref[...]) + b_ref[...]


def _modulation(cond8, w_ada, b_ada):
    n = w_ada.shape[1]
    tn = 1024
    return pl.pallas_call(
        _mod_kernel,
        grid=(n // tn,),
        in_specs=[
            pl.BlockSpec((8, D_MODEL), lambda j: (0, 0)),
            pl.BlockSpec((D_MODEL, tn), lambda j: (0, j)),
            pl.BlockSpec((1, tn), lambda j: (0, j)),
        ],
        out_specs=pl.BlockSpec((8, tn), lambda j: (0, j)),
        out_shape=jax.ShapeDtypeStruct((8, n), F32),
        compiler_params=_cparams(("parallel",)),
        name="modulation",
    )(cond8, w_ada, b_ada.reshape(1, n))


def _inproj_kernel(x_ref, mod_ref, g_ref, w_ref, o_ref, h_ref):
    @pl.when(pl.program_id(1) == 0)
    def _():
        x = x_ref[...]
        ms = jnp.mean(x * x, axis=-1, keepdims=True)
        y = x * lax.rsqrt(ms + RMS_EPS) * g_ref[...]
        m = mod_ref[...]
        sh1 = m[:, 0:D_MODEL]
        sc1 = m[:, D_MODEL:2 * D_MODEL]
        h_ref[...] = (y * (1.0 + sc1) + sh1).astype(BF16)

    o_ref[...] = _dot(h_ref[...], w_ref[...]).astype(o_ref.dtype)


def _in_projection(x, mod3, mod_base, rows_per_mod, g, w_bf16):
    t = x.shape[0]
    tm, tn = 2048, 768
    n = w_bf16.shape[1]
    return pl.pallas_call(
        _inproj_kernel,
        grid=(t // tm, n // tn),
        in_specs=[
            pl.BlockSpec((tm, D_MODEL), lambda i, j: (i, 0)),
            pl.BlockSpec((None, 1, 6 * D_MODEL), lambda i, j: (mod_base + (i * tm) // rows_per_mod, 0, 0)),
            pl.BlockSpec((1, D_MODEL), lambda i, j: (0, 0)),
            pl.BlockSpec((D_MODEL, tn), lambda i, j: (0, j)),
        ],
        out_specs=pl.BlockSpec((tm, tn), lambda i, j: (i, j)),
        out_shape=jax.ShapeDtypeStruct((t, n), BF16),
        scratch_shapes=[pltpu.VMEM((tm, D_MODEL), BF16)],
        compiler_params=_cparams(("parallel", "arbitrary")),
        name="in_projection",
    )(x, mod3, g.reshape(1, D_MODEL), w_bf16)


def _shift_seq_kernel(seq_len, p_ref, mu_ref, dir_ref, o_ref):
    x = p_ref[...].astype(F32)
    tr = x.shape[0]
    pos = lax.broadcasted_iota(jnp.int32, x.shape, 0) % seq_len
    prev = jnp.where(pos == 0, 0.0, pltpu.roll(x, 1, axis=0))
    nxt = jnp.where(pos == seq_len - 1, 0.0, pltpu.roll(x, tr - 1, axis=0))
    shifted = jnp.where(dir_ref[...] == 0, prev, nxt)
    o_ref[...] = x + mu_ref[...] * (shifted - x)


def _shift_grid_kernel(seq_len, p_ref, hp_ref, hn_ref, mu_ref, dir_ref, o_ref):
    x = p_ref[...].astype(F32)
    tr = x.shape[0]
    i = pl.program_id(0)
    tiles_per_seq = seq_len // tr
    first = (i % tiles_per_seq) == 0
    last = (i % tiles_per_seq) == tiles_per_seq - 1
    col = lax.broadcasted_iota(jnp.int32, x.shape, 0) % GRID_W
    left = jnp.where(col == 0, 0.0, pltpu.roll(x, 1, axis=0))
    right = jnp.where(col == GRID_W - 1, 0.0, pltpu.roll(x, tr - 1, axis=0))
    hp = jnp.where(first, 0.0, hp_ref[...].astype(F32))
    hn = jnp.where(last, 0.0, hn_ref[...].astype(F32))
    up = jnp.concatenate([hp, x[:tr - GRID_W]], axis=0)
    down = jnp.concatenate([x[GRID_W:], hn], axis=0)
    d = dir_ref[...]
    shifted = jnp.where(d == 0, left, jnp.where(d == 1, right, jnp.where(d == 2, up, down)))
    o_ref[...] = x + mu_ref[...] * (shifted - x)


def _dir_codes(widths, n_dirs):
    codes = []
    for w in widths:
        q = w // n_dirs
        codes.append(np.repeat(np.arange(n_dirs, dtype=np.int32), q))
    return np.concatenate(codes)[None, :]


def _token_shift(p, mu_perm, col0, width, group_widths, n_groups, seq_len, grid_mode):
    t = p.shape[0]
    tr = 512
    cb0 = col0 // width
    n_dirs = 4 if grid_mode else 2
    dirs = jnp.asarray(_dir_codes(group_widths, n_dirs))
    main = pl.BlockSpec((tr, width), lambda i, g: (i, cb0 + g))
    mu_spec = pl.BlockSpec((1, width), lambda i, g: (0, cb0 + g))
    dir_spec = pl.BlockSpec((1, width), lambda i, g: (0, 0))
    out_spec = pl.BlockSpec((None, tr, width), lambda i, g: (g, i, 0))
    out_shape = jax.ShapeDtypeStruct((n_groups, t, width), F32)
    if not grid_mode:
        return pl.pallas_call(
            functools.partial(_shift_seq_kernel, seq_len),
            grid=(t // tr, n_groups),
            in_specs=[main, mu_spec, dir_spec],
            out_specs=out_spec,
            out_shape=out_shape,
            compiler_params=_cparams(("parallel", "parallel")),
            name="token_shift_seq",
        )(p, mu_perm, dirs)
    hb = tr // GRID_W
    n_hb = t // GRID_W
    halo_prev = pl.BlockSpec((GRID_W, width), lambda i, g: (jnp.maximum(i * hb - 1, 0), cb0 + g))
    halo_next = pl.BlockSpec((GRID_W, width), lambda i, g: (jnp.minimum((i + 1) * hb, n_hb - 1), cb0 + g))
    return pl.pallas_call(
        functools.partial(_shift_grid_kernel, seq_len),
        grid=(t // tr, n_groups),
        in_specs=[main, halo_prev, halo_next, mu_spec, dir_spec],
        out_specs=out_spec,
        out_shape=out_shape,
        compiler_params=_cparams(("parallel", "parallel")),
        name="token_shift_grid",
    )(p, p, p, mu_perm, dirs)


def _bd_rows(x):
    c = x.shape[0]
    xt = jnp.concatenate([x, x, x, x], axis=0)
    rb = lax.broadcasted_iota(jnp.int32, xt.shape, 0) // c
    lb = lax.broadcasted_iota(jnp.int32, xt.shape, 1) // HEAD_DIM
    return jnp.where(rb == lb, xt, jnp.zeros_like(xt))


def _dot_nt(a, b):
    return lax.dot_general(a, b, (((1,), (1,)), ((), ())), preferred_element_type=F32)


def _dot_tn(a, b):
    return lax.dot_general(a, b, (((0,), (0,)), ((), ())), preferred_element_type=F32)


def _wkv_chains(chains, ones_bd):
    c = chains[0]["r"].shape[0]
    ti = lax.broadcasted_iota(jnp.int32, (c, c), 0)
    si = lax.broadcasted_iota(jnp.int32, (c, c), 1)
    tri = {rev: jnp.where((si >= ti) if rev else (si <= ti), 1.0, 0.0).astype(BF16) for rev in (False, True)}
    tri3 = {rev: jnp.concatenate([tri[rev]] * 3, axis=1) for rev in (False, True)}
    t2 = lax.broadcasted_iota(jnp.int32, (c, 4 * c), 0)
    s2 = lax.broadcasted_iota(jnp.int32, (c, 4 * c), 1) % c
    strict = {False: s2 < t2, True: s2 > t2}
    incl = {False: s2 <= t2, True: s2 >= t2}
    eye = jnp.where(s2 == t2, 1.0, 0.0)

    for ch in chains:
        ch["kkr"] = ch["k"] * ch["kk_w"]
    for ch in chains:
        ch["ss"] = _dot_exact_rhs(ch["kkr"] * ch["kkr"], ones_bd)
    for ch in chains:
        ch["wlin"] = ch["w0"] + _dot3(jnp.tanh(ch["sm"][:, 0:64]), ch["wup"])
    for ch in chains:
        ch["a"] = _sigmoid(ch["a0"] + _dot3(ch["sm"][:, 64:128], ch["aup"]))
    for ch in chains:
        logw = -jnp.exp(-_softplus(-ch["wlin"]) - 0.5)
        lh = logw.astype(BF16)
        l1 = logw - lh.astype(F32)
        lm = l1.astype(BF16)
        ll = (l1 - lm.astype(F32)).astype(BF16)
        ch["logw"] = logw
        ch["cum"] = _dot(tri3[ch["rev"]], jnp.concatenate([lh, lm, ll], axis=0))
    for ch in chains:
        rev, cum, k = ch["rev"], ch["cum"], ch["k"]
        kk = ch["kkr"] * lax.rsqrt(jnp.maximum(ch["ss"], 1e-24))
        a = ch["a"]
        kd = k * (1.0 + (a - 1.0) * ch["ka_w"])
        b = kk * a
        ltot = cum[0:1, :] if rev else cum[c - 1:c, :]
        e_neg = jnp.exp(-cum)
        e_tail = jnp.exp(ltot - cum)
        at = -kk * jnp.exp(cum - ch["logw"])
        rt = ch["r"] * jnp.exp(cum)
        ch["ltot"] = ltot
        ch["ar"] = jnp.concatenate([at, rt], axis=0).astype(BF16)
        ch["bt"] = _bd_rows(b * e_neg).astype(BF16)
        ch["kt"] = _bd_rows(kd * e_neg).astype(BF16)
        ch["bk"] = jnp.concatenate([b * e_tail, kd * e_tail], axis=0).astype(BF16)
    for ch in chains:
        ch["m_b"] = _dot_nt(ch["ar"], ch["bt"])
    for ch in chains:
        ch["m_k"] = _dot_nt(ch["ar"], ch["kt"])
    for ch in chains:
        ch["st"] = _dot_nt(ch["ar"], ch["s0"].astype(BF16))
    for ch in chains:
        rev = ch["rev"]
        ch["m_ab"] = jnp.where(strict[rev], ch["m_b"][:c], 0.0)
        ch["n_rb"] = jnp.where(incl[rev], ch["m_b"][c:], 0.0).astype(BF16)
        ch["n_rk"] = jnp.where(incl[rev], ch["m_k"][c:], 0.0).astype(BF16)
        ch["v_bd"] = _bd_rows(ch["v"]).astype(BF16)
    for ch in chains:
        m_ak = jnp.where(strict[ch["rev"]], ch["m_k"][:c], 0.0).astype(BF16)
        ch["rhs"] = ch["st"][:c] + _dot(m_ak, ch["v_bd"])

    blk = 1
    while blk < c:
        tb = t2 // blk
        sb = s2 // blk
        off = {False: (tb % 2 == 1) & (sb == tb - 1), True: (tb % 2 == 0) & (sb == tb + 1)}
        if blk == 1:
            for ch in chains:
                ch["x"] = eye + jnp.where(off[ch["rev"]], ch["m_ab"], 0.0)
        else:
            for ch in chains:
                m_off = jnp.where(off[ch["rev"]], ch["m_ab"], 0.0)
                ch["p1"] = _dot(ch["x"].astype(BF16), _bd_rows(m_off).astype(BF16))
            for ch in chains:
                ch["x"] = ch["x"] + _dot(ch["p1"].astype(BF16), _bd_rows(ch["x"]).astype(BF16))
        blk *= 2

    for ch in chains:
        ch["u"] = _dot(ch["x"].astype(BF16), _bd_rows(ch["rhs"]).astype(BF16))
    for ch in chains:
        u_bd = _bd_rows(ch["u"]).astype(BF16)
        ch["o"] = ch["st"][c:] + _dot(ch["n_rb"], u_bd) + _dot(ch["n_rk"], ch["v_bd"])
    out = []
    for ch in chains:
        uv = jnp.concatenate([ch["u"], ch["v"]], axis=0).astype(BF16)
        upd = _dot_tn(uv, ch["bk"])
        out.append((ch["o"], ch["s0"] * jnp.exp(ch["ltot"]) + jnp.where(ones_bd > 0, upd, 0.0)))
    return out


def _wkv_kernel(zero_init, nc,
                rf_ref, kf_ref, vf_ref, smf_ref, rb_ref, kb_ref, vb_ref, smb_ref,
                kkw_ref, kaw_ref, w0_ref, wup_ref, a0_ref, aup_ref, s0_ref,
                of_ref, ob_ref, sout_ref, sf_ref, sb_ref):
    ci = pl.program_id(2)
    ones_bd = _head_ones(HG)
    n_g = sf_ref.shape[0]

    @pl.when(ci == 0)
    def _():
        sf_ref[...] = jnp.zeros_like(sf_ref)
        sb_ref[...] = jnp.zeros_like(sb_ref)
        if not zero_init:
            for d, ref in ((0, sf_ref), (1, sb_ref)):
                for h in range(4 * n_g):
                    sl = slice((h % 4) * HEAD_DIM, (h % 4 + 1) * HEAD_DIM)
                    ref[h // 4, sl, sl] = s0_ref[d, h]

    smf = smf_ref[...]
    smb = smb_ref[...]
    chains = []
    for g in range(n_g):
        ln = slice(g * HG, (g + 1) * HG)
        common = dict(kk_w=kkw_ref[:, ln], ka_w=kaw_ref[:, ln])
        chains.append(dict(rev=False, r=rf_ref[:, ln], k=kf_ref[:, ln], v=vf_ref[:, ln], sm=smf,
                           w0=w0_ref[0, :, ln], wup=wup_ref[0, :, ln], a0=a0_ref[0, :, ln],
                           aup=aup_ref[0, :, ln], s0=sf_ref[g], **common))
        chains.append(dict(rev=True, r=rb_ref[:, ln], k=kb_ref[:, ln], v=vb_ref[:, ln], sm=smb,
                           w0=w0_ref[1, :, ln], wup=wup_ref[1, :, ln], a0=a0_ref[1, :, ln],
                           aup=aup_ref[1, :, ln], s0=sb_ref[g], **common))
    results = _wkv_chains(chains, ones_bd)
    for g in range(n_g):
        ln = slice(g * HG, (g + 1) * HG)
        (o_f, s_f), (o_b, s_b) = results[2 * g], results[2 * g + 1]
        of_ref[:, ln] = o_f
        ob_ref[:, ln] = o_b
        sf_ref[g] = s_f
        sb_ref[g] = s_b

    @pl.when(ci == nc - 1)
    def _():
        for d, ref in ((0, sf_ref), (1, sb_ref)):
            for h in range(4 * n_g):
                sl = slice((h % 4) * HEAD_DIM, (h % 4 + 1) * HEAD_DIM)
                sout_ref[d, h] = ref[h // 4, sl, sl]


def _wkv_scan(rkv, small, n_seq, seq_len, s0, wts):
    t = rkv.shape[1]
    c = WKV_CHUNK
    nc = seq_len // c
    gs = WKV_GROUPS_PER_STEP
    wl = gs * HG
    ng = D_MODEL // wl
    zero_init = s0 is None
    if zero_init:
        s0 = jnp.zeros((1, 2, N_HEADS, HEAD_DIM, HEAD_DIM), F32)

    def fwd(which):
        return pl.BlockSpec((None, c, wl), lambda b, g, ci: (which, b * nc + ci, g))

    def bwd(which):
        return pl.BlockSpec((None, c, wl), lambda b, g, ci: (which, b * nc + (nc - 1 - ci), g))

    sm_f = pl.BlockSpec((None, c, 256), lambda b, g, ci: (0, b * nc + ci, 0))
    sm_b = pl.BlockSpec((None, c, 256), lambda b, g, ci: (0, b * nc + (nc - 1 - ci), 0))
    vec = pl.BlockSpec((1, wl), lambda b, g, ci: (0, g))
    vec2 = pl.BlockSpec((2, 1, wl), lambda b, g, ci: (0, 0, g))
    up2 = pl.BlockSpec((2, 64, wl), lambda b, g, ci: (0, 0, g))
    if zero_init:
        s0_spec = pl.BlockSpec((None, 2, 4 * gs, HEAD_DIM, HEAD_DIM), lambda b, g, ci: (0, 0, g, 0, 0))
    else:
        s0_spec = pl.BlockSpec((None, 2, 4 * gs, HEAD_DIM, HEAD_DIM), lambda b, g, ci: (b, 0, g, 0, 0))
    o_f = pl.BlockSpec((c, wl), lambda b, g, ci: (b * nc + ci, g))
    o_b = pl.BlockSpec((c, wl), lambda b, g, ci: (b * nc + (nc - 1 - ci), g))
    s_out = pl.BlockSpec((None, 2, 4 * gs, HEAD_DIM, HEAD_DIM), lambda b, g, ci: (b, 0, g, 0, 0))
    return pl.pallas_call(
        functools.partial(_wkv_kernel, zero_init, nc),
        grid=(n_seq, ng, nc),
        in_specs=[fwd(0), fwd(1), fwd(2), sm_f, bwd(0), bwd(1), bwd(2), sm_b,
                  vec, vec, vec2, up2, vec2, up2, s0_spec],
        out_specs=[o_f, o_b, s_out],
        out_shape=[jax.ShapeDtypeStruct((t, D_MODEL), F32), jax.ShapeDtypeStruct((t, D_MODEL), F32),
                   jax.ShapeDtypeStruct((n_seq, 2, N_HEADS, HEAD_DIM, HEAD_DIM), F32)],
        scratch_shapes=[pltpu.VMEM((gs, HG, HG), F32), pltpu.VMEM((gs, HG, HG), F32)],
        compiler_params=_cparams(("parallel", "parallel", "arbitrary")),
        name="wkv_scan",
    )(rkv, rkv, rkv, small, rkv, rkv, rkv, small,
      wts["k_k"], wts["k_a"], wts["w0"], wts["w_up"], wts["a0"], wts["a_up"], s0)


LRU_CH = 256
LRU_TILE = 384
LRU_HALO = 16


def _gelu_tanh(x):
    return 0.5 * x * (1.0 + jnp.tanh(0.7978845608028654 * (x + 0.044715 * (x * x * x))))


def _lru_kernel(seq_len, n_rows, lx_ref, lg_ref, cw_ref, cb_ref, wg_ref, bg_ref, lam_ref, h0_ref,
                y_ref, hfin_ref):
    n_ch = n_rows // LRU_CH
    hfin_ref[...] = jnp.zeros_like(hfin_ref)
    cw = cw_ref[...]
    cb = cb_ref[...]

    def chunk(d, ci, hc):
        rev = d == 1
        start = pl.multiple_of((n_ch - 1 - ci if rev else ci) * LRU_CH, LRU_CH)
        at_seq_start = (start % seq_len) == 0
        at_seq_end = ((start + LRU_CH) % seq_len) == 0
        hr = LRU_HALO
        prev = lx_ref[pl.ds(pl.multiple_of(jnp.maximum(start - hr, 0), hr), hr), :].astype(F32)
        nxt = lx_ref[pl.ds(pl.multiple_of(jnp.minimum(start + LRU_CH, n_rows - hr), hr), hr), :].astype(F32)
        prev = jnp.where(at_seq_start, 0.0, prev)
        nxt = jnp.where(at_seq_end, 0.0, nxt)
        cur = lx_ref[pl.ds(start, LRU_CH), :].astype(F32)
        ext = jnp.concatenate([prev, cur, nxt], axis=0)
        n_ext = LRU_CH + 2 * hr
        xm2 = pltpu.roll(ext, 2, axis=0)[hr:hr + LRU_CH]
        xm1 = pltpu.roll(ext, 1, axis=0)[hr:hr + LRU_CH]
        xp1 = pltpu.roll(ext, n_ext - 1, axis=0)[hr:hr + LRU_CH]
        xb = cb + xm2 * cw[0:1] + xm1 * cw[1:2] + cur * cw[2:3] + xp1 * cw[3:4]
        pre = _dot(xb.astype(BF16), wg_ref[d]) + bg_ref[d]
        r_g = _sigmoid(pre[:, :LRU_TILE])
        i_g = _sigmoid(pre[:, LRU_TILE:])
        log_a = (LRU_C * r_g) * (-_softplus(-lam_ref[d]))
        a_val = jnp.exp(log_a)
        th = jnp.tanh(log_a)
        q = -2.0 * th / (1.0 - th)
        root = jnp.where(q > 0.0, q * lax.rsqrt(q), 0.0)
        u_val = root * i_g * xb

        n_grp = LRU_CH // 8
        a_val = a_val.reshape(n_grp, 8, LRU_TILE)
        u_val = u_val.reshape(n_grp, 8, LRU_TILE)
        sub = lax.broadcasted_iota(jnp.int32, (n_grp, 8, LRU_TILE), 1)
        for s in (1, 2, 4):
            shift = 8 - s if rev else s
            valid = (sub < 8 - s) if rev else (sub >= s)
            a_sh = pltpu.roll(a_val, shift, axis=1)
            u_sh = pltpu.roll(u_val, shift, axis=1)
            u_val = jnp.where(valid, a_val * u_sh + u_val, u_val)
            a_val = jnp.where(valid, a_val * a_sh, a_val)
        a_val = a_val.reshape(LRU_CH, LRU_TILE)
        u_val = u_val.reshape(LRU_CH, LRU_TILE)
        hc = jnp.where(at_seq_end if rev else at_seq_start, h0_ref[d], hc)
        pieces = [None] * n_grp
        for j in (range(n_grp - 1, -1, -1) if rev else range(n_grp)):
            h = u_val[8 * j:8 * j + 8] + a_val[8 * j:8 * j + 8] * hc
            pieces[j] = h
            hc = h[0:1] if rev else h[7:8]
        h_all = jnp.concatenate(pieces, axis=0)
        rows = pl.ds(start, LRU_CH)
        if rev:
            y_ref[rows, :] = (y_ref[rows, :] + h_all) * _gelu_tanh(lg_ref[rows, :].astype(F32))
        else:
            y_ref[rows, :] = h_all
        seq_i = start // seq_len

        @pl.when(at_seq_start if rev else at_seq_end)
        def _():
            hfin_ref[d, pl.ds(seq_i, 1), :] = hc

        return hc

    for d in range(2):
        lax.fori_loop(0, n_ch, functools.partial(chunk, d), jnp.zeros((1, LRU_TILE), F32))


def _lru_scan(p, n_seq, seq_len, h0, wts):
    t = p.shape[0]
    n_rows = max(seq_len, 8 * LRU_CH)
    nb = t // n_rows
    nt = D_RNN // LRU_TILE
    chain = h0 is not None
    if not chain:
        h0 = jnp.zeros((1, 2, 1, D_RNN), F32)
        h0_spec = pl.BlockSpec((None, 2, 1, LRU_TILE), lambda i, j: (0, 0, 0, j))
    else:
        h0 = h0.reshape(n_seq, 2, 1, D_RNN)
        h0_spec = pl.BlockSpec((None, 2, 1, LRU_TILE), lambda i, j: (i, 0, 0, j))
    lg0 = COL_LG // LRU_TILE
    y, hfin = pl.pallas_call(
        functools.partial(_lru_kernel, seq_len, n_rows),
        grid=(nb, nt),
        in_specs=[
            pl.BlockSpec((n_rows, LRU_TILE), lambda i, j: (i, j)),
            pl.BlockSpec((n_rows, LRU_TILE), lambda i, j: (i, lg0 + j)),
            pl.BlockSpec((CONV_W, LRU_TILE), lambda i, j: (0, j)),
            pl.BlockSpec((1, LRU_TILE), lambda i, j: (0, j)),
            pl.BlockSpec((2, None, LRU_TILE, 2 * LRU_TILE), lambda i, j: (0, j, 0, 0)),
            pl.BlockSpec((2, None, 1, 2 * LRU_TILE), lambda i, j: (0, j, 0, 0)),
            pl.BlockSpec((2, 1, LRU_TILE), lambda i, j: (0, 0, j)),
            h0_spec,
        ],
        out_specs=[
            pl.BlockSpec((n_rows, LRU_TILE), lambda i, j: (i, j)),
            pl.BlockSpec((2, None, 8, LRU_TILE), lambda i, j: (0, i, 0, j)),
        ],
        out_shape=[jax.ShapeDtypeStruct((t, D_RNN), F32), jax.ShapeDtypeStruct((2, nb, 8, D_RNN), F32)],
        compiler_params=_cparams(("parallel", "parallel")),
        name="lru_scan",
    )(p, p, wts["conv_w"], wts["conv_b"], wts["lru_wg"], wts["lru_bg"], wts["lru_lam"], h0)
    return y, hfin


def _lru_gate_weights(wa, ba, wx, bx):
    nt = D_RNN // LRU_TILE
    per = LRU_TILE // LRU_BLOCK
    eye = jnp.eye(per, dtype=F32)

    def tiles(w):
        w = w.reshape(2, nt, per, LRU_BLOCK, LRU_BLOCK)
        bd = jnp.einsum("dtaij,ab->dtaibj", w, eye)
        return bd.reshape(2, nt, LRU_TILE, LRU_TILE)

    wg = jnp.concatenate([tiles(wa), tiles(wx)], axis=-1).astype(BF16)
    bg = jnp.concatenate([ba.reshape(2, nt, 1, LRU_TILE), bx.reshape(2, nt, 1, LRU_TILE)], axis=-1)
    return wg, bg


ROUTER_LANES = 128
ROUTE_TILE = 256
PACK_W = 256


def _pack_bf16_pairs(x):
    bits = pltpu.bitcast(x.astype(BF16).astype(F32), jnp.uint32)
    hi_mask = jnp.uint32(0xFFFF0000)

    def pack(hi, lo):
        return (hi & hi_mask) | (lo >> 16)

    return (pack(bits[:, 0:PACK_W], bits[:, PACK_W:2 * PACK_W]),
            pack(bits[:, 2 * PACK_W:3 * PACK_W], bits[:, 3 * PACK_W:4 * PACK_W]))


def _unpack_bf16_pairs(pa, pb):
    hi_mask = jnp.uint32(0xFFFF0000)
    parts = []
    for p in (pa, pb):
        parts.append(pltpu.bitcast(p & hi_mask, F32))
        parts.append(pltpu.bitcast(p << 16, F32))
    return jnp.concatenate(parts, axis=1)


def _head_sum(x, ones_bd):
    parts = [_dot_exact_rhs(x[:, g * HG:(g + 1) * HG], ones_bd) for g in range(x.shape[1] // HG)]
    return jnp.concatenate(parts, axis=1)


def _merge_kernel(o0_ref, o1_ref, rkv_ref, sm_ref, ga_ref, gb_ref, yb_ref, x_ref, mod_ref,
                  aup_ref, a0_ref, ka_ref, rk_ref, gup_ref, lng_ref, lnb_ref, proj_ref, bm_ref,
                  wout_ref, g2_ref, rw_ref, rb_ref, x1_ref, ha_ref, hb_ref, route_ref, cnt_ref):
    ones_bd = _head_ones(HG)
    o = o0_ref[...] + o1_ref[...]
    mu = _head_sum(o, ones_bd) * (1.0 / HEAD_DIM)
    oc = o - mu
    var = _head_sum(oc * oc, ones_bd) * (1.0 / HEAD_DIM)
    on = oc * lax.rsqrt(var + GN_EPS)
    r = rkv_ref[0]
    k = rkv_ref[1]
    v = rkv_ref[2]
    sm = sm_ref[...]
    ad = sm[:, 64:128]
    gd = sm[:, 128:256]
    ka = ka_ref[...]
    kd_rk = jnp.zeros_like(k)
    for d in range(2):
        a = _sigmoid(a0_ref[d] + _dot3(ad, aup_ref[d]))
        kd_rk = kd_rk + k * (1.0 + (a - 1.0) * ka) * rk_ref[d]
    bonus = _head_sum(r * kd_rk, ones_bd) * v
    g = _dot(_sigmoid(gd).astype(BF16), gup_ref[...])
    y_a = (on * lng_ref[...] + lnb_ref[...] + bonus) * g
    y_b = _dot(yb_ref[...].astype(BF16), proj_ref[...])
    bm = bm_ref[...]
    gate_a = _sigmoid(ga_ref[...].astype(F32) + bm[:, :D_MODEL])
    gate_b = _sigmoid(gb_ref[...].astype(F32) + bm[:, D_MODEL:])
    y = gate_a * y_a + gate_b * y_b
    m = mod_ref[...]
    gt1 = m[:, 2 * D_MODEL:3 * D_MODEL]
    sh2 = m[:, 3 * D_MODEL:4 * D_MODEL]
    sc2 = m[:, 4 * D_MODEL:5 * D_MODEL]
    x1 = x_ref[...] + gt1 * _dot(y.astype(BF16), wout_ref[...])
    x1_ref[...] = x1
    ms = jnp.mean(x1 * x1, axis=-1, keepdims=True)
    h2 = (x1 * lax.rsqrt(ms + RMS_EPS) * g2_ref[...]) * (1.0 + sc2) + sh2
    ha_ref[...], hb_ref[...] = _pack_bf16_pairs(h2)

    logits = _dot3(h2, rw_ref[...]) + rb_ref[...]
    lane = lax.broadcasted_iota(jnp.int32, logits.shape, 1)
    work = logits
    vals, sels, firsts = [], [], []
    for _ in range(TOP_K):
        mx = jnp.max(work, axis=-1, keepdims=True)
        first = jnp.min(jnp.where(work == mx, lane, ROUTER_LANES), axis=-1, keepdims=True)
        sel = lane == first
        vals.append(mx)
        sels.append(sel)
        firsts.append(first)
        work = jnp.where(sel, -jnp.inf, work)
    es = [jnp.exp(val - vals[0]) for val in vals]
    inv = 1.0 / (es[0] + es[1] + es[2] + es[3])
    mask = jnp.where(sels[0] | sels[1] | sels[2] | sels[3], 1.0, 0.0)
    tm = mask.shape[0]
    ri = lax.broadcasted_iota(jnp.int32, (tm, tm), 0)
    ci = lax.broadcasted_iota(jnp.int32, (tm, tm), 1)
    earlier = jnp.where(ci < ri, 1.0, 0.0).astype(BF16)
    rank = _dot(earlier, mask.astype(BF16))
    route = jnp.zeros_like(logits)
    for j in range(TOP_K):
        rank_j = jnp.sum(jnp.where(sels[j], rank, 0.0), axis=-1, keepdims=True)
        route = jnp.where(lane == j, firsts[j].astype(F32), route)
        route = jnp.where(lane == TOP_K + j, es[j] * inv, route)
        route = jnp.where(lane == 2 * TOP_K + j, rank_j, route)
    route_ref[...] = route
    cnt_ref[...] = jnp.broadcast_to(jnp.sum(mask, axis=0, keepdims=True), cnt_ref.shape)


def _merge(o0, o1, rkv, small, p, yb, x, mod3, mod_base, rows_per_mod, w):
    t = x.shape[0]
    tm = ROUTE_TILE

    def row(width, col=0):
        return pl.BlockSpec((tm, width), lambda i: (i, col))

    def full(shape):
        nd = len(shape)
        return pl.BlockSpec(shape, lambda i: (0,) * nd)

    return pl.pallas_call(
        _merge_kernel,
        grid=(t // tm,),
        in_specs=[
            row(D_MODEL), row(D_MODEL),
            pl.BlockSpec((3, tm, D_MODEL), lambda i: (0, i, 0)),
            pl.BlockSpec((None, tm, 256), lambda i: (0, i, 0)),
            row(D_MODEL, COL_GA // D_MODEL), row(D_MODEL, COL_GB // D_MODEL),
            row(D_RNN), row(D_MODEL),
            pl.BlockSpec((None, 1, 6 * D_MODEL), lambda i: (mod_base + (i * tm) // rows_per_mod, 0, 0)),
            full((2, 64, D_MODEL)), full((2, 1, D_MODEL)), full((1, D_MODEL)), full((2, 1, D_MODEL)),
            full((128, D_MODEL)), full((1, D_MODEL)), full((1, D_MODEL)), full((D_RNN, D_MODEL)),
            full((1, 2 * D_MODEL)), full((D_MODEL, D_MODEL)), full((1, D_MODEL)),
            full((D_MODEL, ROUTER_LANES)), full((1, ROUTER_LANES)),
        ],
        out_specs=[row(D_MODEL), row(PACK_W), row(PACK_W), row(ROUTER_LANES),
                   pl.BlockSpec((None, 8, ROUTER_LANES), lambda i: (i, 0, 0))],
        out_shape=[jax.ShapeDtypeStruct((t, D_MODEL), F32),
                   jax.ShapeDtypeStruct((t, PACK_W), jnp.uint32), jax.ShapeDtypeStruct((t, PACK_W), jnp.uint32),
                   jax.ShapeDtypeStruct((t, ROUTER_LANES), F32),
                   jax.ShapeDtypeStruct((t // tm, 8, ROUTER_LANES), F32)],
        compiler_params=_cparams(("parallel",)),
        name="merge",
    )(o0, o1, rkv, small, p, p, yb, x, mod3,
      w["a_up"], w["a0"], w["k_a"], w["r_k"], w["g_up"], w["ln_g"], w["ln_b"], w["lru_proj"],
      w["b_merge"], w["w_out"], w["norm_ffn_g"], w["router_w"], w["router_b"])


EXPERT_TILE = 512
SC_WINDOW = 128


def _sc_mesh():
    return plsc.VectorSubcoreMesh(core_axis_name="core", subcore_axis_name="subcore")


def _sc_scatter_rows(x, pos_flat, n_out):
    t, width = x.shape
    n_idx = pos_flat.shape[1]
    n_src = t // SC_WINDOW

    @pl.kernel(out_type=jax.ShapeDtypeStruct((n_out, width), x.dtype), mesh=_sc_mesh(), scratch_types=[])
    def scatter(x_hbm, i_hbm, o_hbm):
        def body(x_vmem, i_vmem):
            pltpu.sync_copy(x_vmem, o_hbm.at[i_vmem.at[0]])

        pltpu.emit_pipeline(
            body,
            grid=(n_idx // SC_WINDOW,),
            in_specs=[pl.BlockSpec((SC_WINDOW, width), index_map=lambda i: (i % n_src, 0)),
                      pl.BlockSpec((1, SC_WINDOW), index_map=lambda i: (0, i))],
            out_specs=[],
            core_axis_name=("core", "subcore"),
            dimension_semantics=(pltpu.PARALLEL,),
        )(x_hbm, i_hbm)

    return scatter(x, pos_flat)


def _sc_gather_rows(y, pos_flat):
    n_idx = pos_flat.shape[1]
    width = y.shape[1]

    @pl.kernel(out_type=jax.ShapeDtypeStruct((n_idx, width), y.dtype), mesh=_sc_mesh(), scratch_types=[])
    def gather(y_hbm, i_hbm, o_hbm):
        def body(i_vmem, o_vmem):
            pltpu.sync_copy(y_hbm.at[i_vmem.at[0]], o_vmem)

        pltpu.emit_pipeline(
            body,
            grid=(n_idx // SC_WINDOW,),
            in_specs=[pl.BlockSpec((1, SC_WINDOW), index_map=lambda i: (0, i))],
            out_specs=[pl.BlockSpec((SC_WINDOW, width), index_map=lambda i: (i, 0))],
            core_axis_name=("core", "subcore"),
            dimension_semantics=(pltpu.PARALLEL,),
        )(i_hbm, o_hbm)

    return gather(y, pos_flat)


def _expert_kernel(te_ref, nt_ref, xa_ref, xb_ref, wgu_ref, bgu_ref, wd_ref, bd_ref, ya_ref, yb_ref,
                   wgu_bf, wd_bf):
    i = pl.program_id(0)
    prev = te_ref[jnp.maximum(i - 1, 0)]
    fresh = (i == 0) | (te_ref[i] != prev)

    @pl.when(fresh)
    def _():
        wgu_bf[...] = wgu_ref[...].astype(BF16)
        wd_bf[...] = wd_ref[...].astype(BF16)

    @pl.when(i < nt_ref[0])
    def _():
        d_exp = wd_bf.shape[0]
        bgu = bgu_ref[...]
        half = EXPERT_TILE // 2
        for s in range(2):
            rows = pl.ds(s * half, half)
            x = _unpack_bf16_pairs(xa_ref[rows, :], xb_ref[rows, :]).astype(BF16)
            gate = _dot(x, wgu_bf[:, :d_exp]) + bgu[:, :d_exp]
            up = _dot(x, wgu_bf[:, d_exp:]) + bgu[:, d_exp:]
            gate = jnp.minimum(gate, SWIGLU_LIMIT)
            up = jnp.clip(up, -SWIGLU_LIMIT, SWIGLU_LIMIT)
            act = gate * _sigmoid(SWIGLU_ALPHA * gate) * (up + 1.0)
            y = _dot(act.astype(BF16), wd_bf[...]) + bd_ref[...]
            ya_ref[rows, :], yb_ref[rows, :] = _pack_bf16_pairs(y)


def _experts(xs_a, xs_b, tile_expert, n_tiles_used, w):
    n_rows = xs_a.shape[0]
    n_tiles = n_rows // EXPERT_TILE
    n_exp, _, two_d = w["exp_w_gu"].shape
    d_exp = two_d // 2

    def rows(i, te, nt):
        return (jnp.minimum(i, nt[0] - 1), 0)

    def by_expert(i, te, nt):
        return (te[i], 0, 0)

    grid_spec = pltpu.PrefetchScalarGridSpec(
        num_scalar_prefetch=2,
        grid=(n_tiles,),
        in_specs=[
            pl.BlockSpec((EXPERT_TILE, PACK_W), rows),
            pl.BlockSpec((EXPERT_TILE, PACK_W), rows),
            pl.BlockSpec((None, D_MODEL, two_d), by_expert),
            pl.BlockSpec((None, 1, two_d), by_expert),
            pl.BlockSpec((None, d_exp, D_MODEL), by_expert),
            pl.BlockSpec((None, 1, D_MODEL), by_expert),
        ],
        out_specs=[pl.BlockSpec((EXPERT_TILE, PACK_W), rows), pl.BlockSpec((EXPERT_TILE, PACK_W), rows)],
        scratch_shapes=[pltpu.VMEM((D_MODEL, two_d), BF16), pltpu.VMEM((d_exp, D_MODEL), BF16)],
    )
    return pl.pallas_call(
        _expert_kernel,
        grid_spec=grid_spec,
        out_shape=[jax.ShapeDtypeStruct((n_rows, PACK_W), jnp.uint32)] * 2,
        compiler_params=_cparams(("arbitrary",)),
        name="experts",
    )(tile_expert, n_tiles_used, xs_a, xs_b, w["exp_w_gu"], w["exp_b_gu"], w["exp_w_down"], w["exp_b_down"])


def _combine_kernel(ya_ref, yb_ref, route_ref, x1_ref, mod_ref, fg_ref, o_ref):
    route = route_ref[...]
    lane = lax.broadcasted_iota(jnp.int32, route.shape, 1)
    acc = jnp.zeros(x1_ref.shape, F32)
    for j in range(TOP_K):
        w_j = jnp.sum(jnp.where(lane == TOP_K + j, route, 0.0), axis=-1, keepdims=True)
        acc = acc + w_j * _unpack_bf16_pairs(ya_ref[j], yb_ref[j])
    gt2 = mod_ref[...][:, 5 * D_MODEL:6 * D_MODEL]
    x2 = x1_ref[...] + gt2 * acc
    ms = jnp.mean(x2 * x2, axis=-1, keepdims=True)
    o_ref[...] = x2 * lax.rsqrt(ms + RMS_EPS) * fg_ref[...]


def _combine(yg_a, yg_b, route, x1, mod3, mod_base, rows_per_mod, row0, w):
    t = x1.shape[0]
    tm = ROUTE_TILE
    b0 = row0 // tm
    return pl.pallas_call(
        _combine_kernel,
        grid=(t // tm,),
        in_specs=[
            pl.BlockSpec((TOP_K, tm, PACK_W), lambda i: (0, b0 + i, 0)),
            pl.BlockSpec((TOP_K, tm, PACK_W), lambda i: (0, b0 + i, 0)),
            pl.BlockSpec((tm, ROUTER_LANES), lambda i: (b0 + i, 0)),
            pl.BlockSpec((tm, D_MODEL), lambda i: (i, 0)),
            pl.BlockSpec((None, 1, 6 * D_MODEL), lambda i: (mod_base + (i * tm) // rows_per_mod, 0, 0)),
            pl.BlockSpec((1, D_MODEL), lambda i: (0, 0)),
        ],
        out_specs=pl.BlockSpec((tm, D_MODEL), lambda i: (i, 0)),
        out_shape=jax.ShapeDtypeStruct((t, D_MODEL), F32),
        compiler_params=_cparams(("parallel",)),
        name="combine",
    )(yg_a, yg_b, route, x1, mod3, w["final_norm_g"])


def _routing_tables(route, cnt, n_exp):
    t = route.shape[0]
    e4 = route[:, 0:TOP_K].astype(jnp.int32)
    r4 = route[:, 2 * TOP_K:3 * TOP_K].astype(jnp.int32)
    cnt = cnt[:, 0, :n_exp].astype(jnp.int32)
    tile_off = jnp.cumsum(cnt, axis=0) - cnt
    n_e = jnp.sum(cnt, axis=0)
    region = (n_e + EXPERT_TILE - 1) // EXPERT_TILE * EXPERT_TILE
    region_end = jnp.cumsum(region)
    base = region_end - region
    seg_start = jnp.repeat(base[None, :] + tile_off, ROUTE_TILE, axis=0)
    onehot = e4[:, :, None] == jnp.arange(n_exp, dtype=jnp.int32)[None, None, :]
    pos4 = jnp.sum(jnp.where(onehot, seg_start[:, None, :], 0), axis=-1) + r4
    n_tiles_max = (TOP_K * t + n_exp * (EXPERT_TILE - 1)) // EXPERT_TILE
    tile_start = jnp.arange(n_tiles_max, dtype=jnp.int32) * EXPERT_TILE
    regions_before = jnp.sum(tile_start[:, None] >= region_end[None, :], axis=1)
    tile_expert = jnp.minimum(regions_before, n_exp - 1)
    n_used = (region_end[-1] // EXPERT_TILE).reshape(1)
    return pos4.T.reshape(1, TOP_K * t), tile_expert.astype(jnp.int32), n_used.astype(jnp.int32), n_tiles_max


_N_SHIFT = 3328
_PERM = np.concatenate([
    np.arange(_N_SHIFT, _N_SHIFT + 2 * D_RNN),
    np.arange(0, 3 * D_MODEL),
    np.arange(_N_SHIFT + 2 * D_RNN, _N_SHIFT + 2 * D_RNN + 2 * D_MODEL),
    np.arange(3 * D_MODEL, _N_SHIFT),
])


def _stream(x, n_seq, seq_len, mod3, mod_base, rows_per_mod, s_wkv0, s_lru0, grid_mode, w):
    t = n_seq * seq_len
    x2 = x.reshape(t, D_MODEL)
    p = _in_projection(x2, mod3, mod_base, rows_per_mod, w["norm_mix_g"], w["w_in"])
    rkv = _token_shift(p, w["mu"], COL_R, D_MODEL, (D_MODEL,), 3, seq_len, grid_mode)
    small = _token_shift(p, w["mu"], COL_SMALL, 256, (64, 64, 128), 1, seq_len, grid_mode)
    o0, o1, s_wkv = _wkv_scan(rkv, small, n_seq, seq_len, s_wkv0, w)
    yb, s_lru = _lru_scan(p, n_seq, seq_len, s_lru0, w)
    x1, h_a, h_b, route, cnt = _merge(o0, o1, rkv, small, p, yb, x2, mod3, mod_base, rows_per_mod, w)
    return dict(x1=x1, h_a=h_a, h_b=h_b, route=route, cnt=cnt, s_wkv=s_wkv, s_lru=s_lru)


def _moe_and_norm(streams, mods, n_exp, w):
    h_a = jnp.concatenate([s["h_a"] for s in streams], axis=0)
    h_b = jnp.concatenate([s["h_b"] for s in streams], axis=0)
    route = jnp.concatenate([s["route"] for s in streams], axis=0)
    cnt = jnp.concatenate([s["cnt"] for s in streams], axis=0)
    t_all = route.shape[0]
    pos_flat, tile_expert, n_used, n_tiles_max = _routing_tables(route, cnt, n_exp)
    n_rows = n_tiles_max * EXPERT_TILE
    xs_a = _sc_scatter_rows(h_a, pos_flat, n_rows)
    xs_b = _sc_scatter_rows(h_b, pos_flat, n_rows)
    ys_a, ys_b = _experts(xs_a, xs_b, tile_expert, n_used, w)
    yg_a = _sc_gather_rows(ys_a, pos_flat).reshape(TOP_K, t_all, PACK_W)
    yg_b = _sc_gather_rows(ys_b, pos_flat).reshape(TOP_K, t_all, PACK_W)
    outs, row0 = [], 0
    for s, (mod3, mod_base, rows_per_mod) in zip(streams, mods):
        outs.append(_combine(yg_a, yg_b, route, s["x1"], mod3, mod_base, rows_per_mod, row0, w))
        row0 += s["x1"].shape[0]
    return outs


def kernel(x_prompt, x_sample, state_wkv, state_lru, c, c_ctx, norm_mix_g, norm_ffn_g, w_ada, b_ada, w_in, shift_mu, wkv_k_k, wkv_k_a, wkv_r_k, wkv_w0, wkv_w_up, wkv_a0, wkv_a_up, wkv_g_up, wkv_ln_g, wkv_ln_b, lru_conv_w, lru_conv_b, lru_wa, lru_ba, lru_wx, lru_bx, lru_lambda, lru_proj, b_merge, w_out, router_w, router_b, exp_w_gu, exp_b_gu, exp_w_down, exp_b_down, final_norm_g):
    n_ctx, seq, _ = x_prompt.shape
    n_dec, dec_seq, _ = x_sample.shape
    assert w_in.shape[0] == 1, "single trunk layer: the final norm is fused into the MoE kernel"
    assert 1 + n_dec <= 8 and dec_seq % GRID_W == 0
    l = 0
    n_exp = router_w.shape[-1]
    cond8 = jnp.concatenate([c_ctx[None, :], c, jnp.zeros((8 - 1 - n_dec, D_MODEL), F32)], axis=0)
    mod3 = _modulation(cond8, w_ada[l], b_ada[l]).reshape(8, 1, 6 * D_MODEL)
    mu_full = jnp.concatenate([shift_mu[l], jnp.zeros((N_IN - _N_SHIFT,), F32)])
    lru_wg, lru_bg = _lru_gate_weights(lru_wa[l], lru_ba[l], lru_wx[l], lru_bx[l])
    router_w_pad = jnp.zeros((D_MODEL, ROUTER_LANES), F32).at[:, :n_exp].set(router_w[l])
    router_b_pad = jnp.full((1, ROUTER_LANES), -1e30, F32).at[0, :n_exp].set(router_b[l])
    w = dict(
        norm_mix_g=norm_mix_g[l],
        w_in=w_in[l][:, _PERM].astype(BF16),
        mu=mu_full[_PERM].reshape(1, N_IN),
        k_k=wkv_k_k[l].reshape(1, D_MODEL), k_a=wkv_k_a[l].reshape(1, D_MODEL),
        w0=wkv_w0[l].reshape(2, 1, D_MODEL), w_up=wkv_w_up[l],
        a0=wkv_a0[l].reshape(2, 1, D_MODEL), a_up=wkv_a_up[l],
        r_k=wkv_r_k[l].reshape(2, 1, D_MODEL), g_up=wkv_g_up[l].astype(BF16),
        ln_g=wkv_ln_g[l].reshape(1, D_MODEL), ln_b=wkv_ln_b[l].reshape(1, D_MODEL),
        conv_w=lru_conv_w[l], conv_b=lru_conv_b[l].reshape(1, D_RNN),
        lru_wg=lru_wg, lru_bg=lru_bg, lru_lam=lru_lambda[l].reshape(2, 1, D_RNN),
        lru_proj=lru_proj[l].astype(BF16), b_merge=b_merge[l].reshape(1, 2 * D_MODEL),
        w_out=w_out[l].astype(BF16), norm_ffn_g=norm_ffn_g[l].reshape(1, D_MODEL),
        router_w=router_w_pad, router_b=router_b_pad,
        exp_w_gu=exp_w_gu[l], exp_b_gu=exp_b_gu[l].reshape(n_exp, 1, -1),
        exp_w_down=exp_w_down[l], exp_b_down=exp_b_down[l].reshape(n_exp, 1, D_MODEL),
        final_norm_g=final_norm_g.reshape(1, D_MODEL),
    )
    ctx = _stream(x_prompt, n_ctx, seq, mod3, 0, n_ctx * seq, None, None, False, w)
    smp = _stream(x_sample, n_dec, dec_seq, mod3, 1, dec_seq, state_wkv[:, l], state_lru[:, l], True, w)
    y_prompt, y_sample = _moe_and_norm([ctx, smp], [(mod3, 0, n_ctx * seq), (mod3, 1, dec_seq)], n_exp, w)
    new_lru = ctx["s_lru"].reshape(2, n_ctx, D_RNN).transpose(1, 0, 2)
    return (y_prompt.reshape(x_prompt.shape), y_sample.reshape(x_sample.shape),
            ctx["s_wkv"][:, None].astype(state_wkv.dtype), new_lru[:, None].astype(state_lru.dtype))
```

```python
import functools

import numpy as np
import jax
import jax.numpy as jnp
from jax import lax
from jax.experimental import pallas as pl
from jax.experimental.pallas import tpu as pltpu
from jax.experimental.pallas import tpu_sc as plsc

F32 = jnp.float32
BF16 = jnp.bfloat16

D_MODEL = 1024
HEAD_DIM = 64
N_HEADS = 16
GRID_W = 64
D_RNN = 1536
LRU_BLOCK = 96
CONV_W = 4
LRU_C = 8.0
TOP_K = 4
SWIGLU_LIMIT = 7.0
SWIGLU_ALPHA = 1.702
RMS_EPS = 1e-6
GN_EPS = 64e-5

COL_LX = 0
COL_LG = 1536
COL_R = 3072
COL_K = 4096
COL_V = 5120
COL_GA = 6144
COL_GB = 7168
COL_SMALL = 8192
N_IN = 8448

MXU_DEPTH = 256
HG = 256
WKV_CHUNK = 64
WKV_GROUPS_PER_STEP = 4
VMEM_LIMIT = 56 * 1024 * 1024


def _cparams(sem, vmem=VMEM_LIMIT):
    return pltpu.CompilerParams(dimension_semantics=sem, vmem_limit_bytes=vmem)


def _split2(x):
    hi = x.astype(BF16)
    lo = (x - hi.astype(F32)).astype(BF16)
    return hi, lo


def _dot(a, b):
    return jnp.dot(a, b, preferred_element_type=F32)


def _dot3(a, b):
    ah, al = _split2(a)
    bh, bl = _split2(b)
    if 3 * a.shape[1] <= MXU_DEPTH:
        return _dot(jnp.concatenate([ah, al, ah], axis=1), jnp.concatenate([bh, bh, bl], axis=0))
    return _dot(ah, bh) + (_dot(al, bh) + _dot(ah, bl))


def _dot_exact_rhs(a, b_bf16):
    ah, al = _split2(a)
    m = a.shape[0]
    both = _dot(jnp.concatenate([ah, al], axis=0), b_bf16)
    return both[:m] + both[m:]


def _head_ones(n):
    r = lax.broadcasted_iota(jnp.int32, (n, n), 0) // HEAD_DIM
    c = lax.broadcasted_iota(jnp.int32, (n, n), 1) // HEAD_DIM
    return jnp.where(r == c, 1.0, 0.0).astype(BF16)


def _sigmoid(x):
    return 0.5 * jnp.tanh(0.5 * x) + 0.5


def _softplus(x):
    return jnp.maximum(x, 0.0) + jnp.log(1.0 + jnp.exp(-jnp.abs(x)))


def _mod_kernel(c_ref, w_ref, b_ref, o_ref):
    c = c_ref[...]
    s = c * _sigmoid(c)
    o_ref[...] = _dot3(s, w_ref[...]) + b_ref[...]


def _modulation(cond8, w_ada, b_ada):
    n = w_ada.shape[1]
    tn = 1024
    return pl.pallas_call(
        _mod_kernel,
        grid=(n // tn,),
        in_specs=[
            pl.BlockSpec((8, D_MODEL), lambda j: (0, 0)),
            pl.BlockSpec((D_MODEL, tn), lambda j: (0, j)),
            pl.BlockSpec((1, tn), lambda j: (0, j)),
        ],
        out_specs=pl.BlockSpec((8, tn), lambda j: (0, j)),
        out_shape=jax.ShapeDtypeStruct((8, n), F32),
        compiler_params=_cparams(("parallel",)),
        name="modulation",
    )(cond8, w_ada, b_ada.reshape(1, n))


def _inproj_kernel(x_ref, mod_ref, g_ref, w_ref, o_ref, h_ref):
    @pl.when(pl.program_id(1) == 0)
    def _():
        x = x_ref[...]
        ms = jnp.mean(x * x, axis=-1, keepdims=True)
        y = x * lax.rsqrt(ms + RMS_EPS) * g_ref[...]
        m = mod_ref[...]
        sh1 = m[:, 0:D_MODEL]
        sc1 = m[:, D_MODEL:2 * D_MODEL]
        h_ref[...] = (y * (1.0 + sc1) + sh1).astype(BF16)

    o_ref[...] = _dot(h_ref[...], w_ref[...]).astype(o_ref.dtype)


def _in_projection(x, mod3, mod_base, rows_per_mod, g, w_bf16):
    t = x.shape[0]
    tm, tn = 2048, 768
    n = w_bf16.shape[1]
    return pl.pallas_call(
        _inproj_kernel,
        grid=(t // tm, n // tn),
        in_specs=[
            pl.BlockSpec((tm, D_MODEL), lambda i, j: (i, 0)),
            pl.BlockSpec((None, 1, 6 * D_MODEL), lambda i, j: (mod_base + (i * tm) // rows_per_mod, 0, 0)),
            pl.BlockSpec((1, D_MODEL), lambda i, j: (0, 0)),
            pl.BlockSpec((D_MODEL, tn), lambda i, j: (0, j)),
        ],
        out_specs=pl.BlockSpec((tm, tn), lambda i, j: (i, j)),
        out_shape=jax.ShapeDtypeStruct((t, n), BF16),
        scratch_shapes=[pltpu.VMEM((tm, D_MODEL), BF16)],
        compiler_params=_cparams(("parallel", "arbitrary")),
        name="in_projection",
    )(x, mod3, g.reshape(1, D_MODEL), w_bf16)


def _shift_seq_kernel(seq_len, p_ref, mu_ref, dir_ref, o_ref):
    x = p_ref[...].astype(F32)
    tr = x.shape[0]
    pos = lax.broadcasted_iota(jnp.int32, x.shape, 0) % seq_len
    prev = jnp.where(pos == 0, 0.0, pltpu.roll(x, 1, axis=0))
    nxt = jnp.where(pos == seq_len - 1, 0.0, pltpu.roll(x, tr - 1, axis=0))
    shifted = jnp.where(dir_ref[...] == 0, prev, nxt)
    o_ref[...] = (x + mu_ref[...] * (shifted - x)).astype(o_ref.dtype)


def _shift_grid_kernel(seq_len, p_ref, hp_ref, hn_ref, mu_ref, dir_ref, o_ref):
    x = p_ref[...].astype(F32)
    tr = x.shape[0]
    i = pl.program_id(0)
    tiles_per_seq = seq_len // tr
    first = (i % tiles_per_seq) == 0
    last = (i % tiles_per_seq) == tiles_per_seq - 1
    col = lax.broadcasted_iota(jnp.int32, x.shape, 0) % GRID_W
    left = jnp.where(col == 0, 0.0, pltpu.roll(x, 1, axis=0))
    right = jnp.where(col == GRID_W - 1, 0.0, pltpu.roll(x, tr - 1, axis=0))
    hp = jnp.where(first, 0.0, hp_ref[...].astype(F32))
    hn = jnp.where(last, 0.0, hn_ref[...].astype(F32))
    up = jnp.concatenate([hp, x[:tr - GRID_W]], axis=0)
    down = jnp.concatenate([x[GRID_W:], hn], axis=0)
    d = dir_ref[...]
    shifted = jnp.where(d == 0, left, jnp.where(d == 1, right, jnp.where(d == 2, up, down)))
    o_ref[...] = (x + mu_ref[...] * (shifted - x)).astype(o_ref.dtype)


def _dir_codes(widths, n_dirs):
    codes = []
    for w in widths:
        q = w // n_dirs
        codes.append(np.repeat(np.arange(n_dirs, dtype=np.int32), q))
    return np.concatenate(codes)[None, :]


def _token_shift(p, mu_perm, col0, width, group_widths, n_groups, seq_len, grid_mode, out_dtype):
    t = p.shape[0]
    tr = 512
    cb0 = col0 // width
    n_dirs = 4 if grid_mode else 2
    dirs = jnp.asarray(_dir_codes(group_widths, n_dirs))
    main = pl.BlockSpec((tr, width), lambda i, g: (i, cb0 + g))
    mu_spec = pl.BlockSpec((1, width), lambda i, g: (0, cb0 + g))
    dir_spec = pl.BlockSpec((1, width), lambda i, g: (0, 0))
    out_spec = pl.BlockSpec((None, tr, width), lambda i, g: (g, i, 0))
    out_shape = jax.ShapeDtypeStruct((n_groups, t, width), out_dtype)
    if not grid_mode:
        return pl.pallas_call(
            functools.partial(_shift_seq_kernel, seq_len),
            grid=(t // tr, n_groups),
            in_specs=[main, mu_spec, dir_spec],
            out_specs=out_spec,
            out_shape=out_shape,
            compiler_params=_cparams(("parallel", "parallel")),
            name="token_shift_seq",
        )(p, mu_perm, dirs)
    hb = tr // GRID_W
    n_hb = t // GRID_W
    halo_prev = pl.BlockSpec((GRID_W, width), lambda i, g: (jnp.maximum(i * hb - 1, 0), cb0 + g))
    halo_next = pl.BlockSpec((GRID_W, width), lambda i, g: (jnp.minimum((i + 1) * hb, n_hb - 1), cb0 + g))
    return pl.pallas_call(
        functools.partial(_shift_grid_kernel, seq_len),
        grid=(t // tr, n_groups),
        in_specs=[main, halo_prev, halo_next, mu_spec, dir_spec],
        out_specs=out_spec,
        out_shape=out_shape,
        compiler_params=_cparams(("parallel", "parallel")),
        name="token_shift_grid",
    )(p, p, p, mu_perm, dirs)


def _bd_rows(x):
    c = x.shape[0]
    xt = jnp.concatenate([x, x, x, x], axis=0)
    rb = lax.broadcasted_iota(jnp.int32, xt.shape, 0) // c
    lb = lax.broadcasted_iota(jnp.int32, xt.shape, 1) // HEAD_DIM
    return jnp.where(rb == lb, xt, jnp.zeros_like(xt))


def _dot_nt(a, b):
    return lax.dot_general(a, b, (((1,), (1,)), ((), ())), preferred_element_type=F32)


def _dot_tn(a, b):
    return lax.dot_general(a, b, (((0,), (0,)), ((), ())), preferred_element_type=F32)


def _wkv_chains(chains, ones_bd):
    c = chains[0]["r"].shape[0]
    ti = lax.broadcasted_iota(jnp.int32, (c, c), 0)
    si = lax.broadcasted_iota(jnp.int32, (c, c), 1)
    tri = {rev: jnp.where((si >= ti) if rev else (si <= ti), 1.0, 0.0).astype(BF16) for rev in (False, True)}
    tri3 = {rev: jnp.concatenate([tri[rev]] * 3, axis=1) for rev in (False, True)}
    t2 = lax.broadcasted_iota(jnp.int32, (c, 4 * c), 0)
    s2 = lax.broadcasted_iota(jnp.int32, (c, 4 * c), 1) % c
    strict = {False: s2 < t2, True: s2 > t2}
    incl = {False: s2 <= t2, True: s2 >= t2}
    eye = jnp.where(s2 == t2, 1.0, 0.0)

    for ch in chains:
        ch["kkr"] = ch["k"] * ch["kk_w"]
    sq = jnp.concatenate([ch["kkr"] * ch["kkr"] for ch in chains], axis=0)
    ss_all = _dot_exact_rhs(sq, ones_bd)
    for n, ch in enumerate(chains):
        ch["ss"] = ss_all[n * c:(n + 1) * c]
    for ch in chains:
        ch["wlin"] = ch["w0"] + _dot3(jnp.tanh(ch["sm"][:, 0:64]), ch["wup"])
    for ch in chains:
        ch["a"] = _sigmoid(ch["a0"] + _dot3(ch["sm"][:, 64:128], ch["aup"]))
    for ch in chains:
        logw = -jnp.exp(-_softplus(-ch["wlin"]) - 0.5)
        lh = logw.astype(BF16)
        l1 = logw - lh.astype(F32)
        lm = l1.astype(BF16)
        ll = (l1 - lm.astype(F32)).astype(BF16)
        ch["logw"] = logw
        ch["cum"] = _dot(tri3[ch["rev"]], jnp.concatenate([lh, lm, ll], axis=0))
    for ch in chains:
        rev, cum, k = ch["rev"], ch["cum"], ch["k"]
        kk = ch["kkr"] * lax.rsqrt(jnp.maximum(ch["ss"], 1e-24))
        a = ch["a"]
        kd = k * (1.0 + (a - 1.0) * ch["ka_w"])
        b = kk * a
        ltot = cum[0:1, :] if rev else cum[c - 1:c, :]
        e_neg = jnp.exp(-cum)
        e_tail = jnp.exp(ltot - cum)
        at = -kk * jnp.exp(cum - ch["logw"])
        rt = ch["r"] * jnp.exp(cum)
        ch["ltot"] = ltot
        ch["ar"] = jnp.concatenate([at, rt], axis=0).astype(BF16)
        ch["bt"] = _bd_rows(b * e_neg).astype(BF16)
        ch["kt"] = _bd_rows(kd * e_neg).astype(BF16)
        ch["bk"] = jnp.concatenate([b * e_tail, kd * e_tail], axis=0).astype(BF16)
    for ch in chains:
        ch["m_b"] = _dot_nt(ch["ar"], ch["bt"])
    for ch in chains:
        ch["m_k"] = _dot_nt(ch["ar"], ch["kt"])
    for ch in chains:
        ch["st"] = _dot_nt(ch["ar"], ch["s0"].astype(BF16))
    for ch in chains:
        rev = ch["rev"]
        ch["m_ab"] = jnp.where(strict[rev], ch["m_b"][:c], 0.0)
        ch["n_rb"] = jnp.where(incl[rev], ch["m_b"][c:], 0.0).astype(BF16)
    for ch in chains:
        rev = ch["rev"]
        mk = jnp.concatenate([jnp.where(strict[rev], ch["m_k"][:c], 0.0),
                              jnp.where(incl[rev], ch["m_k"][c:], 0.0)], axis=0).astype(BF16)
        kv = _dot(mk, _bd_rows(ch["v"]).astype(BF16))
        ch["rhs"] = ch["st"][:c] + kv[:c]
        ch["o_v"] = ch["st"][c:] + kv[c:]

    blk = 1
    while blk < c:
        tb = t2 // blk
        sb = s2 // blk
        off = {False: (tb % 2 == 1) & (sb == tb - 1), True: (tb % 2 == 0) & (sb == tb + 1)}
        if blk == 1:
            for ch in chains:
                ch["x"] = eye + jnp.where(off[ch["rev"]], ch["m_ab"], 0.0)
        else:
            for ch in chains:
                m_off = jnp.where(off[ch["rev"]], ch["m_ab"], 0.0)
                ch["p1"] = _dot(ch["x"].astype(BF16), _bd_rows(m_off).astype(BF16))
            for ch in chains:
                ch["x"] = ch["x"] + _dot(ch["p1"].astype(BF16), _bd_rows(ch["x"]).astype(BF16))
        blk *= 2

    for ch in chains:
        ch["u"] = _dot(ch["x"].astype(BF16), _bd_rows(ch["rhs"]).astype(BF16))
    for ch in chains:
        u_bd = _bd_rows(ch["u"]).astype(BF16)
        ch["o"] = ch["o_v"] + _dot(ch["n_rb"], u_bd)
    out = []
    for ch in chains:
        uv = jnp.concatenate([ch["u"], ch["v"]], axis=0).astype(BF16)
        upd = _dot_tn(uv, ch["bk"])
        out.append((ch["o"], ch["s0"] * jnp.exp(ch["ltot"]) + jnp.where(ones_bd > 0, upd, 0.0)))
    return out


def _wkv_kernel(zero_init, nc,
                rf_ref, kf_ref, vf_ref, smf_ref, rb_ref, kb_ref, vb_ref, smb_ref,
                kkw_ref, kaw_ref, w0_ref, wup_ref, a0_ref, aup_ref, s0_ref,
                of_ref, ob_ref, sout_ref, sf_ref, sb_ref):
    ci = pl.program_id(2)
    ones_bd = _head_ones(HG)
    n_g = sf_ref.shape[0]

    @pl.when(ci == 0)
    def _():
        sf_ref[...] = jnp.zeros_like(sf_ref)
        sb_ref[...] = jnp.zeros_like(sb_ref)
        if not zero_init:
            for d, ref in ((0, sf_ref), (1, sb_ref)):
                for h in range(4 * n_g):
                    sl = slice((h % 4) * HEAD_DIM, (h % 4 + 1) * HEAD_DIM)
                    ref[h // 4, sl, sl] = s0_ref[d, h]

    smf = smf_ref[...]
    smb = smb_ref[...]
    chains = []
    for g in range(n_g):
        ln = slice(g * HG, (g + 1) * HG)
        common = dict(kk_w=kkw_ref[:, ln], ka_w=kaw_ref[:, ln])
        chains.append(dict(rev=False, r=rf_ref[:, ln].astype(F32), k=kf_ref[:, ln].astype(F32),
                           v=vf_ref[:, ln].astype(F32), sm=smf,
                           w0=w0_ref[0, :, ln], wup=wup_ref[0, :, ln], a0=a0_ref[0, :, ln],
                           aup=aup_ref[0, :, ln], s0=sf_ref[g], **common))
        chains.append(dict(rev=True, r=rb_ref[:, ln].astype(F32), k=kb_ref[:, ln].astype(F32),
                           v=vb_ref[:, ln].astype(F32), sm=smb,
                           w0=w0_ref[1, :, ln], wup=wup_ref[1, :, ln], a0=a0_ref[1, :, ln],
                           aup=aup_ref[1, :, ln], s0=sb_ref[g], **common))
    results = _wkv_chains(chains, ones_bd)
    for g in range(n_g):
        ln = slice(g * HG, (g + 1) * HG)
        (o_f, s_f), (o_b, s_b) = results[2 * g], results[2 * g + 1]
        of_ref[:, ln] = o_f
        ob_ref[:, ln] = o_b
        sf_ref[g] = s_f
        sb_ref[g] = s_b

    @pl.when(ci == nc - 1)
    def _():
        for d, ref in ((0, sf_ref), (1, sb_ref)):
            for h in range(4 * n_g):
                sl = slice((h % 4) * HEAD_DIM, (h % 4 + 1) * HEAD_DIM)
                sout_ref[d, h] = ref[h // 4, sl, sl]


def _wkv_scan(rkv, small, n_seq, seq_len, s0, wts):
    t = rkv.shape[1]
    c = WKV_CHUNK
    nc = seq_len // c
    gs = WKV_GROUPS_PER_STEP
    wl = gs * HG
    ng = D_MODEL // wl
    zero_init = s0 is None
    if zero_init:
        s0 = jnp.zeros((1, 2, N_HEADS, HEAD_DIM, HEAD_DIM), F32)

    def fwd(which):
        return pl.BlockSpec((None, c, wl), lambda b, g, ci: (which, b * nc + ci, g))

    def bwd(which):
        return pl.BlockSpec((None, c, wl), lambda b, g, ci: (which, b * nc + (nc - 1 - ci), g))

    sm_f = pl.BlockSpec((None, c, 256), lambda b, g, ci: (0, b * nc + ci, 0))
    sm_b = pl.BlockSpec((None, c, 256), lambda b, g, ci: (0, b * nc + (nc - 1 - ci), 0))
    vec = pl.BlockSpec((1, wl), lambda b, g, ci: (0, g))
    vec2 = pl.BlockSpec((2, 1, wl), lambda b, g, ci: (0, 0, g))
    up2 = pl.BlockSpec((2, 64, wl), lambda b, g, ci: (0, 0, g))
    if zero_init:
        s0_spec = pl.BlockSpec((None, 2, 4 * gs, HEAD_DIM, HEAD_DIM), lambda b, g, ci: (0, 0, g, 0, 0))
    else:
        s0_spec = pl.BlockSpec((None, 2, 4 * gs, HEAD_DIM, HEAD_DIM), lambda b, g, ci: (b, 0, g, 0, 0))
    o_f = pl.BlockSpec((c, wl), lambda b, g, ci: (b * nc + ci, g))
    o_b = pl.BlockSpec((c, wl), lambda b, g, ci: (b * nc + (nc - 1 - ci), g))
    s_out = pl.BlockSpec((None, 2, 4 * gs, HEAD_DIM, HEAD_DIM), lambda b, g, ci: (b, 0, g, 0, 0))
    return pl.pallas_call(
        functools.partial(_wkv_kernel, zero_init, nc),
        grid=(n_seq, ng, nc),
        in_specs=[fwd(0), fwd(1), fwd(2), sm_f, bwd(0), bwd(1), bwd(2), sm_b,
                  vec, vec, vec2, up2, vec2, up2, s0_spec],
        out_specs=[o_f, o_b, s_out],
        out_shape=[jax.ShapeDtypeStruct((t, D_MODEL), F32), jax.ShapeDtypeStruct((t, D_MODEL), F32),
                   jax.ShapeDtypeStruct((n_seq, 2, N_HEADS, HEAD_DIM, HEAD_DIM), F32)],
        scratch_shapes=[pltpu.VMEM((gs, HG, HG), F32), pltpu.VMEM((gs, HG, HG), F32)],
        compiler_params=_cparams(("parallel", "parallel", "arbitrary")),
        name="wkv_scan",
    )(rkv, rkv, rkv, small, rkv, rkv, rkv, small,
      wts["k_k"], wts["k_a"], wts["w0"], wts["w_up"], wts["a0"], wts["a_up"], s0)


LRU_CH = 256
LRU_TILE = 384
LRU_HALO = 16


def _gelu_tanh(x):
    return 0.5 * x * (1.0 + jnp.tanh(0.7978845608028654 * (x + 0.044715 * (x * x * x))))


def _lru_group_scan(a_val, u_val, rev):
    n_grp = LRU_CH // 8
    a_val = a_val.reshape(n_grp, 8, LRU_TILE)
    u_val = u_val.reshape(n_grp, 8, LRU_TILE)
    sub = lax.broadcasted_iota(jnp.int32, (n_grp, 8, LRU_TILE), 1)
    for s in (1, 2, 4):
        shift = 8 - s if rev else s
        valid = (sub < 8 - s) if rev else (sub >= s)
        a_sh = pltpu.roll(a_val, shift, axis=1)
        u_sh = pltpu.roll(u_val, shift, axis=1)
        u_val = jnp.where(valid, a_val * u_sh + u_val, u_val)
        a_val = jnp.where(valid, a_val * a_sh, a_val)
    return a_val.reshape(LRU_CH, LRU_TILE), u_val.reshape(LRU_CH, LRU_TILE)


def _lru_carry(a_cum, h_loc, hc, rev):
    n_grp = LRU_CH // 8
    pieces = [None] * n_grp
    for j in (range(n_grp - 1, -1, -1) if rev else range(n_grp)):
        h = h_loc[8 * j:8 * j + 8] + a_cum[8 * j:8 * j + 8] * hc
        pieces[j] = h
        hc = h[0:1] if rev else h[7:8]
    return jnp.concatenate(pieces, axis=0), hc


def _lru_kernel(seq_len, n_rows, lx_ref, lg_ref, cw_ref, cb_ref, wg_ref, bg_ref, lam_ref, h0_ref,
                y_ref, hfin_ref, ac_scr, hl_scr):
    n_ch = n_rows // LRU_CH
    hfin_ref[...] = jnp.zeros_like(hfin_ref)
    cw = cw_ref[...]
    cb = cb_ref[...]

    def seq_edges(start):
        return (start % seq_len) == 0, ((start + LRU_CH) % seq_len) == 0

    def forward_chunk(ci, hc):
        start = pl.multiple_of(ci * LRU_CH, LRU_CH)
        at_seq_start, at_seq_end = seq_edges(start)
        hr = LRU_HALO
        prev = lx_ref[pl.ds(pl.multiple_of(jnp.maximum(start - hr, 0), hr), hr), :].astype(F32)
        nxt = lx_ref[pl.ds(pl.multiple_of(jnp.minimum(start + LRU_CH, n_rows - hr), hr), hr), :].astype(F32)
        prev = jnp.where(at_seq_start, 0.0, prev)
        nxt = jnp.where(at_seq_end, 0.0, nxt)
        cur = lx_ref[pl.ds(start, LRU_CH), :].astype(F32)
        ext = jnp.concatenate([prev, cur, nxt], axis=0)
        n_ext = LRU_CH + 2 * hr
        xm2 = pltpu.roll(ext, 2, axis=0)[hr:hr + LRU_CH]
        xm1 = pltpu.roll(ext, 1, axis=0)[hr:hr + LRU_CH]
        xp1 = pltpu.roll(ext, n_ext - 1, axis=0)[hr:hr + LRU_CH]
        xb = cb + xm2 * cw[0:1] + xm1 * cw[1:2] + cur * cw[2:3] + xp1 * cw[3:4]
        xb_bf = xb.astype(BF16)
        rows = pl.ds(start, LRU_CH)
        for d in range(2):
            pre = _dot(xb_bf, wg_ref[d]) + bg_ref[d]
            r_g = _sigmoid(pre[:, :LRU_TILE])
            i_g = _sigmoid(pre[:, LRU_TILE:])
            log_a = (LRU_C * r_g) * (-_softplus(-lam_ref[d]))
            a_val = jnp.exp(log_a)
            th = jnp.tanh(log_a)
            q = -2.0 * th / (1.0 - th)
            root = jnp.where(q > 0.0, q * lax.rsqrt(q), 0.0)
            a_cum, h_loc = _lru_group_scan(a_val, root * i_g * xb, d == 1)
            if d == 0:
                hc = jnp.where(at_seq_start, h0_ref[0], hc)
                h_all, hc = _lru_carry(a_cum, h_loc, hc, False)
                y_ref[rows, :] = h_all
            else:
                ac_scr[rows, :] = a_cum
                hl_scr[rows, :] = h_loc

        @pl.when(at_seq_end)
        def _():
            hfin_ref[0, pl.ds(start // seq_len, 1), :] = hc

        return hc

    def backward_chunk(ci, hc):
        start = pl.multiple_of((n_ch - 1 - ci) * LRU_CH, LRU_CH)
        at_seq_start, at_seq_end = seq_edges(start)
        rows = pl.ds(start, LRU_CH)
        hc = jnp.where(at_seq_end, h0_ref[1], hc)
        h_all, hc = _lru_carry(ac_scr[rows, :], hl_scr[rows, :], hc, True)
        y_ref[rows, :] = (y_ref[rows, :] + h_all) * _gelu_tanh(lg_ref[rows, :].astype(F32))

        @pl.when(at_seq_start)
        def _():
            hfin_ref[1, pl.ds(start // seq_len, 1), :] = hc

        return hc

    zero = jnp.zeros((1, LRU_TILE), F32)
    lax.fori_loop(0, n_ch, forward_chunk, zero)
    lax.fori_loop(0, n_ch, backward_chunk, zero)


def _lru_scan(p, n_seq, seq_len, h0, wts):
    t = p.shape[0]
    n_rows = max(seq_len, 8 * LRU_CH)
    nb = t // n_rows
    nt = D_RNN // LRU_TILE
    chain = h0 is not None
    if not chain:
        h0 = jnp.zeros((1, 2, 1, D_RNN), F32)
        h0_spec = pl.BlockSpec((None, 2, 1, LRU_TILE), lambda i, j: (0, 0, 0, j))
    else:
        h0 = h0.reshape(n_seq, 2, 1, D_RNN)
        h0_spec = pl.BlockSpec((None, 2, 1, LRU_TILE), lambda i, j: (i, 0, 0, j))
    lg0 = COL_LG // LRU_TILE
    y, hfin = pl.pallas_call(
        functools.partial(_lru_kernel, seq_len, n_rows),
        grid=(nb, nt),
        in_specs=[
            pl.BlockSpec((n_rows, LRU_TILE), lambda i, j: (i, j)),
            pl.BlockSpec((n_rows, LRU_TILE), lambda i, j: (i, lg0 + j)),
            pl.BlockSpec((CONV_W, LRU_TILE), lambda i, j: (0, j)),
            pl.BlockSpec((1, LRU_TILE), lambda i, j: (0, j)),
            pl.BlockSpec((2, None, LRU_TILE, 2 * LRU_TILE), lambda i, j: (0, j, 0, 0)),
            pl.BlockSpec((2, None, 1, 2 * LRU_TILE), lambda i, j: (0, j, 0, 0)),
            pl.BlockSpec((2, 1, LRU_TILE), lambda i, j: (0, 0, j)),
            h0_spec,
        ],
        out_specs=[
            pl.BlockSpec((n_rows, LRU_TILE), lambda i, j: (i, j)),
            pl.BlockSpec((2, None, 8, LRU_TILE), lambda i, j: (0, i, 0, j)),
        ],
        out_shape=[jax.ShapeDtypeStruct((t, D_RNN), F32), jax.ShapeDtypeStruct((2, nb, 8, D_RNN), F32)],
        scratch_shapes=[pltpu.VMEM((n_rows, LRU_TILE), F32), pltpu.VMEM((n_rows, LRU_TILE), F32)],
        compiler_params=_cparams(("parallel", "parallel")),
        name="lru_scan",
    )(p, p, wts["conv_w"], wts["conv_b"], wts["lru_wg"], wts["lru_bg"], wts["lru_lam"], h0)
    return y, hfin


def _lru_gate_weights(wa, ba, wx, bx):
    nt = D_RNN // LRU_TILE
    per = LRU_TILE // LRU_BLOCK
    eye = jnp.eye(per, dtype=F32)

    def tiles(w):
        w = w.reshape(2, nt, per, LRU_BLOCK, LRU_BLOCK)
        bd = jnp.einsum("dtaij,ab->dtaibj", w, eye)
        return bd.reshape(2, nt, LRU_TILE, LRU_TILE)

    wg = jnp.concatenate([tiles(wa), tiles(wx)], axis=-1).astype(BF16)
    bg = jnp.concatenate([ba.reshape(2, nt, 1, LRU_TILE), bx.reshape(2, nt, 1, LRU_TILE)], axis=-1)
    return wg, bg


ROUTER_LANES = 128
ROUTE_TILE = 256
PACK_W = 256


def _pack_bf16_pairs(x):
    bits = pltpu.bitcast(x.astype(BF16).astype(F32), jnp.uint32)
    hi_mask = jnp.uint32(0xFFFF0000)

    def pack(hi, lo):
        return (hi & hi_mask) | (lo >> 16)

    return (pack(bits[:, 0:PACK_W], bits[:, PACK_W:2 * PACK_W]),
            pack(bits[:, 2 * PACK_W:3 * PACK_W], bits[:, 3 * PACK_W:4 * PACK_W]))


def _unpack_bf16_pairs(pa, pb):
    hi_mask = jnp.uint32(0xFFFF0000)
    parts = []
    for p in (pa, pb):
        parts.append(pltpu.bitcast(p & hi_mask, F32))
        parts.append(pltpu.bitcast(p << 16, F32))
    return jnp.concatenate(parts, axis=1)


def _head_sum(x, ones_bd):
    m, n_g = x.shape[0], x.shape[1] // HG
    stacked = jnp.concatenate([x[:, g * HG:(g + 1) * HG] for g in range(n_g)], axis=0)
    sums = _dot_exact_rhs(stacked, ones_bd)
    return jnp.concatenate([sums[g * m:(g + 1) * m] for g in range(n_g)], axis=1)


def _merge_kernel(o0_ref, o1_ref, rkv_ref, sm_ref, ga_ref, gb_ref, yb_ref, x_ref, mod_ref,
                  aup_ref, a0_ref, ka_ref, rk_ref, gup_ref, lng_ref, lnb_ref, proj_ref, bm_ref,
                  wout_ref, g2_ref, rw_ref, rb_ref, x1_ref, ha_ref, hb_ref, route_ref, cnt_ref):
    ones_bd = _head_ones(HG)
    o = o0_ref[...] + o1_ref[...]
    mu = _head_sum(o, ones_bd) * (1.0 / HEAD_DIM)
    oc = o - mu
    var = _head_sum(oc * oc, ones_bd) * (1.0 / HEAD_DIM)
    on = oc * lax.rsqrt(var + GN_EPS)
    r = rkv_ref[0].astype(F32)
    k = rkv_ref[1].astype(F32)
    v = rkv_ref[2].astype(F32)
    sm = sm_ref[...]
    ad = sm[:, 64:128]
    gd = sm[:, 128:256]
    ka = ka_ref[...]
    kd_rk = jnp.zeros_like(k)
    for d in range(2):
        a = _sigmoid(a0_ref[d] + _dot3(ad, aup_ref[d]))
        kd_rk = kd_rk + k * (1.0 + (a - 1.0) * ka) * rk_ref[d]
    bonus = _head_sum(r * kd_rk, ones_bd) * v
    g = _dot(_sigmoid(gd).astype(BF16), gup_ref[...])
    y_a = (on * lng_ref[...] + lnb_ref[...] + bonus) * g
    y_b = _dot(yb_ref[...].astype(BF16), proj_ref[...])
    bm = bm_ref[...]
    gate_a = _sigmoid(ga_ref[...].astype(F32) + bm[:, :D_MODEL])
    gate_b = _sigmoid(gb_ref[...].astype(F32) + bm[:, D_MODEL:])
    y = gate_a * y_a + gate_b * y_b
    m = mod_ref[...]
    gt1 = m[:, 2 * D_MODEL:3 * D_MODEL]
    sh2 = m[:, 3 * D_MODEL:4 * D_MODEL]
    sc2 = m[:, 4 * D_MODEL:5 * D_MODEL]
    x1 = x_ref[...] + gt1 * _dot(y.astype(BF16), wout_ref[...])
    x1_ref[...] = x1
    ms = jnp.mean(x1 * x1, axis=-1, keepdims=True)
    h2 = (x1 * lax.rsqrt(ms + RMS_EPS) * g2_ref[...]) * (1.0 + sc2) + sh2
    ha_ref[...], hb_ref[...] = _pack_bf16_pairs(h2)

    logits = _dot3(h2, rw_ref[...]) + rb_ref[...]
    lane = lax.broadcasted_iota(jnp.int32, logits.shape, 1)
    work = logits
    vals, sels, firsts = [], [], []
    for _ in range(TOP_K):
        mx = jnp.max(work, axis=-1, keepdims=True)
        first = jnp.min(jnp.where(work == mx, lane, ROUTER_LANES), axis=-1, keepdims=True)
        sel = lane == first
        vals.append(mx)
        sels.append(sel)
        firsts.append(first)
        work = jnp.where(sel, -jnp.inf, work)
    es = [jnp.exp(val - vals[0]) for val in vals]
    inv = 1.0 / (es[0] + es[1] + es[2] + es[3])
    mask = jnp.where(sels[0] | sels[1] | sels[2] | sels[3], 1.0, 0.0)
    tm = mask.shape[0]
    ri = lax.broadcasted_iota(jnp.int32, (tm, tm), 0)
    ci = lax.broadcasted_iota(jnp.int32, (tm, tm), 1)
    earlier = jnp.where(ci < ri, 1.0, 0.0).astype(BF16)
    rank = _dot(earlier, mask.astype(BF16))
    route = jnp.zeros_like(logits)
    for j in range(TOP_K):
        rank_j = jnp.sum(jnp.where(sels[j], rank, 0.0), axis=-1, keepdims=True)
        route = jnp.where(lane == j, firsts[j].astype(F32), route)
        route = jnp.where(lane == TOP_K + j, es[j] * inv, route)
        route = jnp.where(lane == 2 * TOP_K + j, rank_j, route)
    route_ref[...] = route
    cnt_ref[...] = jnp.broadcast_to(jnp.sum(mask, axis=0, keepdims=True), cnt_ref.shape)


def _merge(o0, o1, rkv, small, p, yb, x, mod3, mod_base, rows_per_mod, w):
    t = x.shape[0]
    tm = ROUTE_TILE

    def row(width, col=0):
        return pl.BlockSpec((tm, width), lambda i: (i, col))

    def full(shape):
        nd = len(shape)
        return pl.BlockSpec(shape, lambda i: (0,) * nd)

    return pl.pallas_call(
        _merge_kernel,
        grid=(t // tm,),
        in_specs=[
            row(D_MODEL), row(D_MODEL),
            pl.BlockSpec((3, tm, D_MODEL), lambda i: (0, i, 0)),
            pl.BlockSpec((None, tm, 256), lambda i: (0, i, 0)),
            row(D_MODEL, COL_GA // D_MODEL), row(D_MODEL, COL_GB // D_MODEL),
            row(D_RNN), row(D_MODEL),
            pl.BlockSpec((None, 1, 6 * D_MODEL), lambda i: (mod_base + (i * tm) // rows_per_mod, 0, 0)),
            full((2, 64, D_MODEL)), full((2, 1, D_MODEL)), full((1, D_MODEL)), full((2, 1, D_MODEL)),
            full((128, D_MODEL)), full((1, D_MODEL)), full((1, D_MODEL)), full((D_RNN, D_MODEL)),
            full((1, 2 * D_MODEL)), full((D_MODEL, D_MODEL)), full((1, D_MODEL)),
            full((D_MODEL, ROUTER_LANES)), full((1, ROUTER_LANES)),
        ],
        out_specs=[row(D_MODEL), row(PACK_W), row(PACK_W), row(ROUTER_LANES),
                   pl.BlockSpec((None, 8, ROUTER_LANES), lambda i: (i, 0, 0))],
        out_shape=[jax.ShapeDtypeStruct((t, D_MODEL), F32),
                   jax.ShapeDtypeStruct((t, PACK_W), jnp.uint32), jax.ShapeDtypeStruct((t, PACK_W), jnp.uint32),
                   jax.ShapeDtypeStruct((t, ROUTER_LANES), F32),
                   jax.ShapeDtypeStruct((t // tm, 8, ROUTER_LANES), F32)],
        compiler_params=_cparams(("parallel",)),
        name="merge",
    )(o0, o1, rkv, small, p, p, yb, x, mod3,
      w["a_up"], w["a0"], w["k_a"], w["r_k"], w["g_up"], w["ln_g"], w["ln_b"], w["lru_proj"],
      w["b_merge"], w["w_out"], w["norm_ffn_g"], w["router_w"], w["router_b"])


EXPERT_TILE = 528
SC_WINDOW = 128


def _sc_mesh():
    return plsc.VectorSubcoreMesh(core_axis_name="core", subcore_axis_name="subcore")


def _sc_scatter_rows(x, pos_flat, n_out):
    t, width = x.shape
    n_idx = pos_flat.shape[1]
    n_src = t // SC_WINDOW

    @pl.kernel(out_type=jax.ShapeDtypeStruct((n_out, width), x.dtype), mesh=_sc_mesh(), scratch_types=[])
    def scatter(x_hbm, i_hbm, o_hbm):
        def body(x_vmem, i_vmem):
            pltpu.sync_copy(x_vmem, o_hbm.at[i_vmem.at[0]])

        pltpu.emit_pipeline(
            body,
            grid=(n_idx // SC_WINDOW,),
            in_specs=[pl.BlockSpec((SC_WINDOW, width), index_map=lambda i: (i % n_src, 0)),
                      pl.BlockSpec((1, SC_WINDOW), index_map=lambda i: (0, i))],
            out_specs=[],
            core_axis_name=("core", "subcore"),
            dimension_semantics=(pltpu.PARALLEL,),
        )(x_hbm, i_hbm)

    return scatter(x, pos_flat)


def _sc_gather_rows(y, pos_flat):
    n_idx = pos_flat.shape[1]
    width = y.shape[1]

    @pl.kernel(out_type=jax.ShapeDtypeStruct((n_idx, width), y.dtype), mesh=_sc_mesh(), scratch_types=[])
    def gather(y_hbm, i_hbm, o_hbm):
        def body(i_vmem, o_vmem):
            pltpu.sync_copy(y_hbm.at[i_vmem.at[0]], o_vmem)

        pltpu.emit_pipeline(
            body,
            grid=(n_idx // SC_WINDOW,),
            in_specs=[pl.BlockSpec((1, SC_WINDOW), index_map=lambda i: (0, i))],
            out_specs=[pl.BlockSpec((SC_WINDOW, width), index_map=lambda i: (i, 0))],
            core_axis_name=("core", "subcore"),
            dimension_semantics=(pltpu.PARALLEL,),
        )(i_hbm, o_hbm)

    return gather(y, pos_flat)


def _expert_kernel(te_ref, nt_ref, xa_ref, xb_ref, wgu_ref, bgu_ref, wd_ref, bd_ref, ya_ref, yb_ref,
                   wgu_bf, wd_bf):
    i = pl.program_id(0)
    prev = te_ref[jnp.maximum(i - 1, 0)]
    fresh = (i == 0) | (te_ref[i] != prev)

    @pl.when(fresh)
    def _():
        wgu_bf[...] = wgu_ref[...].astype(BF16)
        wd_bf[...] = wd_ref[...].astype(BF16)

    @pl.when(i < nt_ref[0])
    def _():
        d_exp = wd_bf.shape[0]
        bgu = bgu_ref[...]
        x = _unpack_bf16_pairs(xa_ref[...], xb_ref[...]).astype(BF16)
        gate = _dot(x, wgu_bf[:, :d_exp]) + bgu[:, :d_exp]
        up = _dot(x, wgu_bf[:, d_exp:]) + bgu[:, d_exp:]
        gate = jnp.minimum(gate, SWIGLU_LIMIT)
        up = jnp.clip(up, -SWIGLU_LIMIT, SWIGLU_LIMIT)
        act = gate * _sigmoid(SWIGLU_ALPHA * gate) * (up + 1.0)
        y = _dot(act.astype(BF16), wd_bf[...]) + bd_ref[...]
        ya_ref[...], yb_ref[...] = _pack_bf16_pairs(y)


def _experts(xs_a, xs_b, tile_expert, n_tiles_used, w):
    n_rows = xs_a.shape[0]
    n_tiles = n_rows // EXPERT_TILE
    n_exp, _, two_d = w["exp_w_gu"].shape
    d_exp = two_d // 2

    def rows(i, te, nt):
        return (jnp.minimum(i, nt[0] - 1), 0)

    def by_expert(i, te, nt):
        return (te[i], 0, 0)

    grid_spec = pltpu.PrefetchScalarGridSpec(
        num_scalar_prefetch=2,
        grid=(n_tiles,),
        in_specs=[
            pl.BlockSpec((EXPERT_TILE, PACK_W), rows),
            pl.BlockSpec((EXPERT_TILE, PACK_W), rows),
            pl.BlockSpec((None, D_MODEL, two_d), by_expert),
            pl.BlockSpec((None, 1, two_d), by_expert),
            pl.BlockSpec((None, d_exp, D_MODEL), by_expert),
            pl.BlockSpec((None, 1, D_MODEL), by_expert),
        ],
        out_specs=[pl.BlockSpec((EXPERT_TILE, PACK_W), rows), pl.BlockSpec((EXPERT_TILE, PACK_W), rows)],
        scratch_shapes=[pltpu.VMEM((D_MODEL, two_d), BF16), pltpu.VMEM((d_exp, D_MODEL), BF16)],
    )
    return pl.pallas_call(
        _expert_kernel,
        grid_spec=grid_spec,
        out_shape=[jax.ShapeDtypeStruct((n_rows, PACK_W), jnp.uint32)] * 2,
        compiler_params=_cparams(("arbitrary",)),
        name="experts",
    )(tile_expert, n_tiles_used, xs_a, xs_b, w["exp_w_gu"], w["exp_b_gu"], w["exp_w_down"], w["exp_b_down"])


def _combine_kernel(ya_ref, yb_ref, route_ref, x1_ref, mod_ref, fg_ref, o_ref):
    route = route_ref[...]
    lane = lax.broadcasted_iota(jnp.int32, route.shape, 1)
    acc = jnp.zeros(x1_ref.shape, F32)
    for j in range(TOP_K):
        w_j = jnp.sum(jnp.where(lane == TOP_K + j, route, 0.0), axis=-1, keepdims=True)
        acc = acc + w_j * _unpack_bf16_pairs(ya_ref[j], yb_ref[j])
    gt2 = mod_ref[...][:, 5 * D_MODEL:6 * D_MODEL]
    x2 = x1_ref[...] + gt2 * acc
    ms = jnp.mean(x2 * x2, axis=-1, keepdims=True)
    o_ref[...] = x2 * lax.rsqrt(ms + RMS_EPS) * fg_ref[...]


def _combine(yg_a, yg_b, route, x1, mod3, mod_base, rows_per_mod, w):
    t = x1.shape[0]
    tm = ROUTE_TILE
    return pl.pallas_call(
        _combine_kernel,
        grid=(t // tm,),
        in_specs=[
            pl.BlockSpec((TOP_K, tm, PACK_W), lambda i: (0, i, 0)),
            pl.BlockSpec((TOP_K, tm, PACK_W), lambda i: (0, i, 0)),
            pl.BlockSpec((tm, ROUTER_LANES), lambda i: (i, 0)),
            pl.BlockSpec((tm, D_MODEL), lambda i: (i, 0)),
            pl.BlockSpec((None, 1, 6 * D_MODEL), lambda i: (mod_base + (i * tm) // rows_per_mod, 0, 0)),
            pl.BlockSpec((1, D_MODEL), lambda i: (0, 0)),
        ],
        out_specs=pl.BlockSpec((tm, D_MODEL), lambda i: (i, 0)),
        out_shape=jax.ShapeDtypeStruct((t, D_MODEL), F32),
        compiler_params=_cparams(("parallel",)),
        name="combine",
    )(yg_a, yg_b, route, x1, mod3, w["final_norm_g"])


def _routing_tables(route, cnt, n_exp):
    t = route.shape[0]
    e4 = route[:, 0:TOP_K].astype(jnp.int32)
    r4 = route[:, 2 * TOP_K:3 * TOP_K].astype(jnp.int32)
    cnt = cnt[:, 0, :n_exp].astype(jnp.int32)
    tile_off = jnp.cumsum(cnt, axis=0) - cnt
    n_e = jnp.sum(cnt, axis=0)
    region = (n_e + EXPERT_TILE - 1) // EXPERT_TILE * EXPERT_TILE
    region_end = jnp.cumsum(region)
    base = region_end - region
    seg_start = jnp.repeat(base[None, :] + tile_off, ROUTE_TILE, axis=0)
    onehot = e4[:, :, None] == jnp.arange(n_exp, dtype=jnp.int32)[None, None, :]
    pos4 = jnp.sum(jnp.where(onehot, seg_start[:, None, :], 0), axis=-1) + r4
    n_tiles_max = (TOP_K * t + n_exp * (EXPERT_TILE - 1)) // EXPERT_TILE
    tile_start = jnp.arange(n_tiles_max, dtype=jnp.int32) * EXPERT_TILE
    regions_before = jnp.sum(tile_start[:, None] >= region_end[None, :], axis=1)
    tile_expert = jnp.minimum(regions_before, n_exp - 1)
    n_used = (region_end[-1] // EXPERT_TILE).reshape(1)
    return pos4.T, tile_expert.astype(jnp.int32), n_used.astype(jnp.int32), n_tiles_max


_N_SHIFT = 3328
_COL_RANGES = (
    (_N_SHIFT, _N_SHIFT + 2 * D_RNN),
    (0, 3 * D_MODEL),
    (_N_SHIFT + 2 * D_RNN, _N_SHIFT + 2 * D_RNN + 2 * D_MODEL),
    (3 * D_MODEL, _N_SHIFT),
)


def _permute_cols(a):
    return jnp.concatenate([a[..., lo:hi] for lo, hi in _COL_RANGES], axis=-1)


def _stream(x, n_seq, seq_len, mod3, mod_base, rows_per_mod, s_wkv0, s_lru0, grid_mode, w):
    t = n_seq * seq_len
    x2 = x.reshape(t, D_MODEL)
    p = _in_projection(x2, mod3, mod_base, rows_per_mod, w["norm_mix_g"], w["w_in"])
    rkv = _token_shift(p, w["mu"], COL_R, D_MODEL, (D_MODEL,), 3, seq_len, grid_mode, BF16)
    small = _token_shift(p, w["mu"], COL_SMALL, 256, (64, 64, 128), 1, seq_len, grid_mode, F32)
    o0, o1, s_wkv = _wkv_scan(rkv, small, n_seq, seq_len, s_wkv0, w)
    yb, s_lru = _lru_scan(p, n_seq, seq_len, s_lru0, w)
    x1, h_a, h_b, route, cnt = _merge(o0, o1, rkv, small, p, yb, x2, mod3, mod_base, rows_per_mod, w)
    return dict(x1=x1, h_a=h_a, h_b=h_b, route=route, cnt=cnt, s_wkv=s_wkv, s_lru=s_lru)


def _moe_and_norm(streams, mods, n_exp, w):
    h_a = jnp.concatenate([s["h_a"] for s in streams], axis=0)
    h_b = jnp.concatenate([s["h_b"] for s in streams], axis=0)
    route = jnp.concatenate([s["route"] for s in streams], axis=0)
    cnt = jnp.concatenate([s["cnt"] for s in streams], axis=0)
    pos, tile_expert, n_used, n_tiles_max = _routing_tables(route, cnt, n_exp)
    n_rows = n_tiles_max * EXPERT_TILE
    pos_flat = pos.reshape(1, -1)
    xs_a = _sc_scatter_rows(h_a, pos_flat, n_rows)
    xs_b = _sc_scatter_rows(h_b, pos_flat, n_rows)
    ys_a, ys_b = _experts(xs_a, xs_b, tile_expert, n_used, w)
    outs, row0 = [], 0
    for s, (mod3, mod_base, rows_per_mod) in zip(streams, mods):
        t = s["x1"].shape[0]
        pos_s = pos[:, row0:row0 + t].reshape(1, TOP_K * t)
        yg_a = _sc_gather_rows(ys_a, pos_s).reshape(TOP_K, t, PACK_W)
        yg_b = _sc_gather_rows(ys_b, pos_s).reshape(TOP_K, t, PACK_W)
        outs.append(_combine(yg_a, yg_b, s["route"], s["x1"], mod3, mod_base, rows_per_mod, w))
        row0 += t
    return outs


def kernel(x_prompt, x_sample, state_wkv, state_lru, c, c_ctx, norm_mix_g, norm_ffn_g, w_ada, b_ada, w_in, shift_mu, wkv_k_k, wkv_k_a, wkv_r_k, wkv_w0, wkv_w_up, wkv_a0, wkv_a_up, wkv_g_up, wkv_ln_g, wkv_ln_b, lru_conv_w, lru_conv_b, lru_wa, lru_ba, lru_wx, lru_bx, lru_lambda, lru_proj, b_merge, w_out, router_w, router_b, exp_w_gu, exp_b_gu, exp_w_down, exp_b_down, final_norm_g):
    n_ctx, seq, _ = x_prompt.shape
    n_dec, dec_seq, _ = x_sample.shape
    assert w_in.shape[0] == 1, "single trunk layer: the final norm is fused into the MoE kernel"
    assert 1 + n_dec <= 8 and dec_seq % GRID_W == 0
    l = 0
    n_exp = router_w.shape[-1]
    cond8 = jnp.concatenate([c_ctx[None, :], c, jnp.zeros((8 - 1 - n_dec, D_MODEL), F32)], axis=0)
    mod3 = _modulation(cond8, w_ada[l], b_ada[l]).reshape(8, 1, 6 * D_MODEL)
    mu_full = jnp.concatenate([shift_mu[l], jnp.zeros((N_IN - _N_SHIFT,), F32)])
    lru_wg, lru_bg = _lru_gate_weights(lru_wa[l], lru_ba[l], lru_wx[l], lru_bx[l])
    router_w_pad = jnp.zeros((D_MODEL, ROUTER_LANES), F32).at[:, :n_exp].set(router_w[l])
    router_b_pad = jnp.full((1, ROUTER_LANES), -1e30, F32).at[0, :n_exp].set(router_b[l])
    w = dict(
        norm_mix_g=norm_mix_g[l],
        w_in=_permute_cols(w_in[l]).astype(BF16),
        mu=_permute_cols(mu_full).reshape(1, N_IN),
        k_k=wkv_k_k[l].reshape(1, D_MODEL), k_a=wkv_k_a[l].reshape(1, D_MODEL),
        w0=wkv_w0[l].reshape(2, 1, D_MODEL), w_up=wkv_w_up[l],
        a0=wkv_a0[l].reshape(2, 1, D_MODEL), a_up=wkv_a_up[l],
        r_k=wkv_r_k[l].reshape(2, 1, D_MODEL), g_up=wkv_g_up[l].astype(BF16),
        ln_g=wkv_ln_g[l].reshape(1, D_MODEL), ln_b=wkv_ln_b[l].reshape(1, D_MODEL),
        conv_w=lru_conv_w[l], conv_b=lru_conv_b[l].reshape(1, D_RNN),
        lru_wg=lru_wg, lru_bg=lru_bg, lru_lam=lru_lambda[l].reshape(2, 1, D_RNN),
        lru_proj=lru_proj[l].astype(BF16), b_merge=b_merge[l].reshape(1, 2 * D_MODEL),
        w_out=w_out[l].astype(BF16), norm_ffn_g=norm_ffn_g[l].reshape(1, D_MODEL),
        router_w=router_w_pad, router_b=router_b_pad,
        exp_w_gu=exp_w_gu[l], exp_b_gu=exp_b_gu[l].reshape(n_exp, 1, -1),
        exp_w_down=exp_w_down[l], exp_b_down=exp_b_down[l].reshape(n_exp, 1, D_MODEL),
        final_norm_g=final_norm_g.reshape(1, D_MODEL),
    )
    ctx = _stream(x_prompt, n_ctx, seq, mod3, 0, n_ctx * seq, None, None, False, w)
    smp = _stream(x_sample, n_dec, dec_seq, mod3, 1, dec_seq, state_wkv[:, l], state_lru[:, l], True, w)
    y_prompt, y_sample = _moe_and_norm([ctx, smp], [(mod3, 0, n_ctx * seq), (mod3, 1, dec_seq)], n_exp, w)
    new_lru = ctx["s_lru"].reshape(2, n_ctx, D_RNN).transpose(1, 0, 2)
    return (y_prompt.reshape(x_prompt.shape), y_sample.reshape(x_sample.shape),
            ctx["s_wkv"][:, None].astype(state_wkv.dtype), new_lru[:, None].astype(state_lru.dtype))
```

```python
import functools

import numpy as np
import jax
import jax.numpy as jnp
from jax import lax
from jax.experimental import pallas as pl
from jax.experimental.pallas import tpu as pltpu
from jax.experimental.pallas import tpu_sc as plsc

F32 = jnp.float32
BF16 = jnp.bfloat16

D_MODEL = 1024
HEAD_DIM = 64
N_HEADS = 16
GRID_W = 64
D_RNN = 1536
LRU_BLOCK = 96
CONV_W = 4
LRU_C = 8.0
TOP_K = 4
SWIGLU_LIMIT = 7.0
SWIGLU_ALPHA = 1.702
RMS_EPS = 1e-6
GN_EPS = 64e-5

COL_LX = 0
COL_LG = 1536
COL_R = 3072
COL_K = 4096
COL_V = 5120
COL_GA = 6144
COL_GB = 7168
COL_SMALL = 8192
N_IN = 8448

MXU_DEPTH = 256
HG = 256
WKV_CHUNK = 64
WKV_GROUPS_PER_STEP = 4
VMEM_LIMIT = 56 * 1024 * 1024


def _cparams(sem, vmem=VMEM_LIMIT):
    return pltpu.CompilerParams(dimension_semantics=sem, vmem_limit_bytes=vmem)


def _split2(x):
    hi = x.astype(BF16)
    lo = (x - hi.astype(F32)).astype(BF16)
    return hi, lo


def _dot(a, b):
    return jnp.dot(a, b, preferred_element_type=F32)


def _dot3(a, b):
    ah, al = _split2(a)
    bh, bl = _split2(b)
    if 3 * a.shape[1] <= MXU_DEPTH:
        return _dot(jnp.concatenate([ah, al, ah], axis=1), jnp.concatenate([bh, bh, bl], axis=0))
    return _dot(ah, bh) + (_dot(al, bh) + _dot(ah, bl))


def _dot_exact_rhs(a, b_bf16):
    ah, al = _split2(a)
    m = a.shape[0]
    both = _dot(jnp.concatenate([ah, al], axis=0), b_bf16)
    return both[:m] + both[m:]


def _head_ones(n):
    r = lax.broadcasted_iota(jnp.int32, (n, n), 0) // HEAD_DIM
    c = lax.broadcasted_iota(jnp.int32, (n, n), 1) // HEAD_DIM
    return jnp.where(r == c, 1.0, 0.0).astype(BF16)


def _sigmoid(x):
    return 0.5 * jnp.tanh(0.5 * x) + 0.5


def _softplus(x):
    return jnp.maximum(x, 0.0) + jnp.log(1.0 + jnp.exp(-jnp.abs(x)))


def _mod_kernel(c_ref, w_ref, b_ref, o_ref):
    c = c_ref[...]
    s = c * _sigmoid(c)
    o_ref[...] = _dot3(s, w_ref[...]) + b_ref[...]


def _modulation(cond8, w_ada, b_ada):
    n = w_ada.shape[1]
    tn = 1024
    return pl.pallas_call(
        _mod_kernel,
        grid=(n // tn,),
        in_specs=[
            pl.BlockSpec((8, D_MODEL), lambda j: (0, 0)),
            pl.BlockSpec((D_MODEL, tn), lambda j: (0, j)),
            pl.BlockSpec((1, tn), lambda j: (0, j)),
        ],
        out_specs=pl.BlockSpec((8, tn), lambda j: (0, j)),
        out_shape=jax.ShapeDtypeStruct((8, n), F32),
        compiler_params=_cparams(("parallel",)),
        name="modulation",
    )(cond8, w_ada, b_ada.reshape(1, n))


def _inproj_kernel(x_ref, mod_ref, g_ref, w_ref, o_ref, h_ref):
    @pl.when(pl.program_id(1) == 0)
    def _():
        x = x_ref[...]
        ms = jnp.mean(x * x, axis=-1, keepdims=True)
        y = x * lax.rsqrt(ms + RMS_EPS) * g_ref[...]
        m = mod_ref[...]
        sh1 = m[:, 0:D_MODEL]
        sc1 = m[:, D_MODEL:2 * D_MODEL]
        h_ref[...] = (y * (1.0 + sc1) + sh1).astype(BF16)

    o_ref[...] = _dot(h_ref[...], w_ref[...]).astype(o_ref.dtype)


def _in_projection(x, mod3, mod_base, rows_per_mod, g, w_bf16):
    t = x.shape[0]
    tm, tn = 2048, 768
    n = w_bf16.shape[1]
    return pl.pallas_call(
        _inproj_kernel,
        grid=(t // tm, n // tn),
        in_specs=[
            pl.BlockSpec((tm, D_MODEL), lambda i, j: (i, 0)),
            pl.BlockSpec((None, 1, 6 * D_MODEL), lambda i, j: (mod_base + (i * tm) // rows_per_mod, 0, 0)),
            pl.BlockSpec((1, D_MODEL), lambda i, j: (0, 0)),
            pl.BlockSpec((D_MODEL, tn), lambda i, j: (0, j)),
        ],
        out_specs=pl.BlockSpec((tm, tn), lambda i, j: (i, j)),
        out_shape=jax.ShapeDtypeStruct((t, n), BF16),
        scratch_shapes=[pltpu.VMEM((tm, D_MODEL), BF16)],
        compiler_params=_cparams(("parallel", "arbitrary")),
        name="in_projection",
    )(x, mod3, g.reshape(1, D_MODEL), w_bf16)


def _shift_seq_kernel(seq_len, ranges, p_ref, mu_ref, dir_ref, o_ref):
    x = p_ref[...].astype(F32)
    tr = x.shape[0]

    def prev_of(z):
        pos = lax.broadcasted_iota(jnp.int32, z.shape, 0) % seq_len
        return jnp.where(pos == 0, 0.0, pltpu.roll(z, 1, axis=0))

    def next_of(z):
        pos = lax.broadcasted_iota(jnp.int32, z.shape, 0) % seq_len
        return jnp.where(pos == seq_len - 1, 0.0, pltpu.roll(z, tr - 1, axis=0))

    if ranges is None:
        shifted = jnp.where(dir_ref[...] == 0, prev_of(x), next_of(x))
    else:
        shifted = jnp.concatenate([(prev_of if d == 0 else next_of)(x[:, lo:hi]) for lo, hi, d in ranges], axis=1)
    o_ref[...] = (x + mu_ref[...] * (shifted - x)).astype(o_ref.dtype)


def _shift_grid_kernel(seq_len, ranges, p_ref, hp_ref, hn_ref, mu_ref, dir_ref, o_ref):
    x = p_ref[...].astype(F32)
    tr = x.shape[0]
    i = pl.program_id(0)
    tiles_per_seq = seq_len // tr
    first = (i % tiles_per_seq) == 0
    last = (i % tiles_per_seq) == tiles_per_seq - 1

    def left_of(z, lo, hi):
        col = lax.broadcasted_iota(jnp.int32, z.shape, 0) % GRID_W
        return jnp.where(col == 0, 0.0, pltpu.roll(z, 1, axis=0))

    def right_of(z, lo, hi):
        col = lax.broadcasted_iota(jnp.int32, z.shape, 0) % GRID_W
        return jnp.where(col == GRID_W - 1, 0.0, pltpu.roll(z, tr - 1, axis=0))

    def up_of(z, lo, hi):
        hp = jnp.where(first, 0.0, hp_ref[:, lo:hi].astype(F32))
        return jnp.concatenate([hp, z[:tr - GRID_W]], axis=0)

    def down_of(z, lo, hi):
        hn = jnp.where(last, 0.0, hn_ref[:, lo:hi].astype(F32))
        return jnp.concatenate([z[GRID_W:], hn], axis=0)

    fns = (left_of, right_of, up_of, down_of)
    if ranges is None:
        w = x.shape[1]
        d = dir_ref[...]
        shifted = jnp.where(d == 0, left_of(x, 0, w), jnp.where(d == 1, right_of(x, 0, w),
                            jnp.where(d == 2, up_of(x, 0, w), down_of(x, 0, w))))
    else:
        shifted = jnp.concatenate([fns[d](x[:, lo:hi], lo, hi) for lo, hi, d in ranges], axis=1)
    o_ref[...] = (x + mu_ref[...] * (shifted - x)).astype(o_ref.dtype)


def _dir_ranges(widths, n_dirs):
    out, base = [], 0
    for w in widths:
        q = w // n_dirs
        if q % 128:
            return None
        out += [(base + d * q, base + (d + 1) * q, d) for d in range(n_dirs)]
        base += w
    return tuple(out)


def _dir_codes(widths, n_dirs):
    codes = []
    for w in widths:
        q = w // n_dirs
        codes.append(np.repeat(np.arange(n_dirs, dtype=np.int32), q))
    return np.concatenate(codes)[None, :]


def _token_shift(p, mu_perm, col0, width, group_widths, n_groups, seq_len, grid_mode, out_dtype):
    t = p.shape[0]
    tr = 512
    cb0 = col0 // width
    n_dirs = 4 if grid_mode else 2
    dirs = jnp.asarray(_dir_codes(group_widths, n_dirs))
    ranges = _dir_ranges(group_widths, n_dirs)
    main = pl.BlockSpec((tr, width), lambda i, g: (i, cb0 + g))
    mu_spec = pl.BlockSpec((1, width), lambda i, g: (0, cb0 + g))
    dir_spec = pl.BlockSpec((1, width), lambda i, g: (0, 0))
    out_spec = pl.BlockSpec((None, tr, width), lambda i, g: (g, i, 0))
    out_shape = jax.ShapeDtypeStruct((n_groups, t, width), out_dtype)
    if not grid_mode:
        return pl.pallas_call(
            functools.partial(_shift_seq_kernel, seq_len, ranges),
            grid=(t // tr, n_groups),
            in_specs=[main, mu_spec, dir_spec],
            out_specs=out_spec,
            out_shape=out_shape,
            compiler_params=_cparams(("parallel", "parallel")),
            name="token_shift_seq",
        )(p, mu_perm, dirs)
    hb = tr // GRID_W
    n_hb = t // GRID_W
    halo_prev = pl.BlockSpec((GRID_W, width), lambda i, g: (jnp.maximum(i * hb - 1, 0), cb0 + g))
    halo_next = pl.BlockSpec((GRID_W, width), lambda i, g: (jnp.minimum((i + 1) * hb, n_hb - 1), cb0 + g))
    return pl.pallas_call(
        functools.partial(_shift_grid_kernel, seq_len, ranges),
        grid=(t // tr, n_groups),
        in_specs=[main, halo_prev, halo_next, mu_spec, dir_spec],
        out_specs=out_spec,
        out_shape=out_shape,
        compiler_params=_cparams(("parallel", "parallel")),
        name="token_shift_grid",
    )(p, p, p, mu_perm, dirs)


def _bd_rows(x):
    c = x.shape[0]
    xt = jnp.concatenate([x, x, x, x], axis=0)
    rb = lax.broadcasted_iota(jnp.int32, xt.shape, 0) // c
    lb = lax.broadcasted_iota(jnp.int32, xt.shape, 1) // HEAD_DIM
    return jnp.where(rb == lb, xt, jnp.zeros_like(xt))


def _dot_nt(a, b):
    return lax.dot_general(a, b, (((1,), (1,)), ((), ())), preferred_element_type=F32)


def _dot_tn(a, b):
    return lax.dot_general(a, b, (((0,), (0,)), ((), ())), preferred_element_type=F32)


def _wkv_chains(chains, ones_bd):
    c = chains[0]["r"].shape[0]
    ti = lax.broadcasted_iota(jnp.int32, (c, c), 0)
    si = lax.broadcasted_iota(jnp.int32, (c, c), 1)
    tri = {rev: jnp.where((si >= ti) if rev else (si <= ti), 1.0, 0.0).astype(BF16) for rev in (False, True)}
    tri3 = {rev: jnp.concatenate([tri[rev]] * 3, axis=1) for rev in (False, True)}
    t2 = lax.broadcasted_iota(jnp.int32, (c, 4 * c), 0)
    s2 = lax.broadcasted_iota(jnp.int32, (c, 4 * c), 1) % c
    strict = {False: s2 < t2, True: s2 > t2}
    incl = {False: s2 <= t2, True: s2 >= t2}
    eye = jnp.where(s2 == t2, 1.0, 0.0)

    for ch in chains:
        ch["kkr"] = ch["k"] * ch["kk_w"]
    sq = jnp.concatenate([ch["kkr"] * ch["kkr"] for ch in chains], axis=0)
    ss_all = _dot_exact_rhs(sq, ones_bd)
    for n, ch in enumerate(chains):
        ch["ss"] = ss_all[n * c:(n + 1) * c]
    for ch in chains:
        ch["wlin"] = ch["w0"] + _dot3(jnp.tanh(ch["sm"][:, 0:64]), ch["wup"])
    for ch in chains:
        ch["a"] = _sigmoid(ch["a0"] + _dot3(ch["sm"][:, 64:128], ch["aup"]))
    for ch in chains:
        logw = -jnp.exp(-_softplus(-ch["wlin"]) - 0.5)
        lh = logw.astype(BF16)
        l1 = logw - lh.astype(F32)
        lm = l1.astype(BF16)
        ll = (l1 - lm.astype(F32)).astype(BF16)
        ch["logw"] = logw
        ch["cum"] = _dot(tri3[ch["rev"]], jnp.concatenate([lh, lm, ll], axis=0))
    for ch in chains:
        rev, cum, k = ch["rev"], ch["cum"], ch["k"]
        kk = ch["kkr"] * lax.rsqrt(jnp.maximum(ch["ss"], 1e-24))
        a = ch["a"]
        kd = k * (1.0 + (a - 1.0) * ch["ka_w"])
        b = kk * a
        ltot = cum[0:1, :] if rev else cum[c - 1:c, :]
        e_neg = jnp.exp(-cum)
        e_tail = jnp.exp(ltot - cum)
        at = -kk * jnp.exp(cum - ch["logw"])
        rt = ch["r"] * jnp.exp(cum)
        ch["ltot"] = ltot
        ch["ar"] = jnp.concatenate([at, rt], axis=0).astype(BF16)
        ch["bt"] = _bd_rows(b * e_neg).astype(BF16)
        ch["kt"] = _bd_rows(kd * e_neg).astype(BF16)
        ch["bk"] = jnp.concatenate([b * e_tail, kd * e_tail], axis=0).astype(BF16)
    for ch in chains:
        ch["m_b"] = _dot_nt(ch["ar"], ch["bt"])
    for ch in chains:
        ch["m_k"] = _dot_nt(ch["ar"], ch["kt"])
    for ch in chains:
        ch["st"] = _dot_nt(ch["ar"], ch["s0"].astype(BF16))
    for ch in chains:
        rev = ch["rev"]
        ch["m_ab"] = jnp.where(strict[rev], ch["m_b"][:c], 0.0)
        ch["n_rb"] = jnp.where(incl[rev], ch["m_b"][c:], 0.0).astype(BF16)
    for ch in chains:
        rev = ch["rev"]
        mk = jnp.concatenate([jnp.where(strict[rev], ch["m_k"][:c], 0.0),
                              jnp.where(incl[rev], ch["m_k"][c:], 0.0)], axis=0).astype(BF16)
        kv = _dot(mk, _bd_rows(ch["v"]).astype(BF16))
        ch["rhs"] = ch["st"][:c] + kv[:c]
        ch["o_v"] = ch["st"][c:] + kv[c:]

    blk = 1
    while blk < c:
        tb = t2 // blk
        sb = s2 // blk
        off = {False: (tb % 2 == 1) & (sb == tb - 1), True: (tb % 2 == 0) & (sb == tb + 1)}
        if blk == 1:
            for ch in chains:
                ch["x"] = eye + jnp.where(off[ch["rev"]], ch["m_ab"], 0.0)
        else:
            for ch in chains:
                m_off = jnp.where(off[ch["rev"]], ch["m_ab"], 0.0)
                ch["p1"] = _dot(ch["x"].astype(BF16), _bd_rows(m_off).astype(BF16))
            for ch in chains:
                ch["x"] = ch["x"] + _dot(ch["p1"].astype(BF16), _bd_rows(ch["x"]).astype(BF16))
        blk *= 2

    for ch in chains:
        ch["u"] = _dot(ch["x"].astype(BF16), _bd_rows(ch["rhs"]).astype(BF16))
    for ch in chains:
        u_bd = _bd_rows(ch["u"]).astype(BF16)
        ch["o"] = ch["o_v"] + _dot(ch["n_rb"], u_bd)
    out = []
    for ch in chains:
        uv = jnp.concatenate([ch["u"], ch["v"]], axis=0).astype(BF16)
        upd = _dot_tn(uv, ch["bk"])
        out.append((ch["o"], ch["s0"] * jnp.exp(ch["ltot"]) + jnp.where(ones_bd > 0, upd, 0.0)))
    return out


def _wkv_kernel(zero_init, nc,
                rf_ref, kf_ref, vf_ref, smf_ref, rb_ref, kb_ref, vb_ref, smb_ref,
                kkw_ref, kaw_ref, w0_ref, wup_ref, a0_ref, aup_ref, s0_ref,
                of_ref, ob_ref, sout_ref, sf_ref, sb_ref):
    ci = pl.program_id(2)
    ones_bd = _head_ones(HG)
    n_g = sf_ref.shape[0]

    @pl.when(ci == 0)
    def _():
        sf_ref[...] = jnp.zeros_like(sf_ref)
        sb_ref[...] = jnp.zeros_like(sb_ref)
        if not zero_init:
            for d, ref in ((0, sf_ref), (1, sb_ref)):
                for h in range(4 * n_g):
                    sl = slice((h % 4) * HEAD_DIM, (h % 4 + 1) * HEAD_DIM)
                    ref[h // 4, sl, sl] = s0_ref[d, h]

    smf = smf_ref[...]
    smb = smb_ref[...]
    chains = []
    for g in range(n_g):
        ln = slice(g * HG, (g + 1) * HG)
        common = dict(kk_w=kkw_ref[:, ln], ka_w=kaw_ref[:, ln])
        chains.append(dict(rev=False, r=rf_ref[:, ln].astype(F32), k=kf_ref[:, ln].astype(F32),
                           v=vf_ref[:, ln].astype(F32), sm=smf,
                           w0=w0_ref[0, :, ln], wup=wup_ref[0, :, ln], a0=a0_ref[0, :, ln],
                           aup=aup_ref[0, :, ln], s0=sf_ref[g], **common))
        chains.append(dict(rev=True, r=rb_ref[:, ln].astype(F32), k=kb_ref[:, ln].astype(F32),
                           v=vb_ref[:, ln].astype(F32), sm=smb,
                           w0=w0_ref[1, :, ln], wup=wup_ref[1, :, ln], a0=a0_ref[1, :, ln],
                           aup=aup_ref[1, :, ln], s0=sb_ref[g], **common))
    results = _wkv_chains(chains, ones_bd)
    for g in range(n_g):
        ln = slice(g * HG, (g + 1) * HG)
        (o_f, s_f), (o_b, s_b) = results[2 * g], results[2 * g + 1]
        of_ref[:, ln] = o_f
        ob_ref[:, ln] = o_b
        sf_ref[g] = s_f
        sb_ref[g] = s_b

    @pl.when(ci == nc - 1)
    def _():
        for d, ref in ((0, sf_ref), (1, sb_ref)):
            for h in range(4 * n_g):
                sl = slice((h % 4) * HEAD_DIM, (h % 4 + 1) * HEAD_DIM)
                sout_ref[d, h] = ref[h // 4, sl, sl]


def _wkv_scan(rkv, small, n_seq, seq_len, s0, wts):
    t = rkv.shape[1]
    c = WKV_CHUNK
    nc = seq_len // c
    gs = WKV_GROUPS_PER_STEP
    wl = gs * HG
    ng = D_MODEL // wl
    zero_init = s0 is None
    if zero_init:
        s0 = jnp.zeros((1, 2, N_HEADS, HEAD_DIM, HEAD_DIM), F32)

    def fwd(which):
        return pl.BlockSpec((None, c, wl), lambda b, g, ci: (which, b * nc + ci, g))

    def bwd(which):
        return pl.BlockSpec((None, c, wl), lambda b, g, ci: (which, b * nc + (nc - 1 - ci), g))

    sm_f = pl.BlockSpec((None, c, 256), lambda b, g, ci: (0, b * nc + ci, 0))
    sm_b = pl.BlockSpec((None, c, 256), lambda b, g, ci: (0, b * nc + (nc - 1 - ci), 0))
    vec = pl.BlockSpec((1, wl), lambda b, g, ci: (0, g))
    vec2 = pl.BlockSpec((2, 1, wl), lambda b, g, ci: (0, 0, g))
    up2 = pl.BlockSpec((2, 64, wl), lambda b, g, ci: (0, 0, g))
    if zero_init:
        s0_spec = pl.BlockSpec((None, 2, 4 * gs, HEAD_DIM, HEAD_DIM), lambda b, g, ci: (0, 0, g, 0, 0))
    else:
        s0_spec = pl.BlockSpec((None, 2, 4 * gs, HEAD_DIM, HEAD_DIM), lambda b, g, ci: (b, 0, g, 0, 0))
    o_f = pl.BlockSpec((c, wl), lambda b, g, ci: (b * nc + ci, g))
    o_b = pl.BlockSpec((c, wl), lambda b, g, ci: (b * nc + (nc - 1 - ci), g))
    s_out = pl.BlockSpec((None, 2, 4 * gs, HEAD_DIM, HEAD_DIM), lambda b, g, ci: (b, 0, g, 0, 0))
    return pl.pallas_call(
        functools.partial(_wkv_kernel, zero_init, nc),
        grid=(n_seq, ng, nc),
        in_specs=[fwd(0), fwd(1), fwd(2), sm_f, bwd(0), bwd(1), bwd(2), sm_b,
                  vec, vec, vec2, up2, vec2, up2, s0_spec],
        out_specs=[o_f, o_b, s_out],
        out_shape=[jax.ShapeDtypeStruct((t, D_MODEL), F32), jax.ShapeDtypeStruct((t, D_MODEL), F32),
                   jax.ShapeDtypeStruct((n_seq, 2, N_HEADS, HEAD_DIM, HEAD_DIM), F32)],
        scratch_shapes=[pltpu.VMEM((gs, HG, HG), F32), pltpu.VMEM((gs, HG, HG), F32)],
        compiler_params=_cparams(("parallel", "parallel", "arbitrary")),
        name="wkv_scan",
    )(rkv, rkv, rkv, small, rkv, rkv, rkv, small,
      wts["k_k"], wts["k_a"], wts["w0"], wts["w_up"], wts["a0"], wts["a_up"], s0)


LRU_CH = 256
LRU_TILE = 384
LRU_HALO = 16


def _gelu_tanh(x):
    return 0.5 * x * (1.0 + jnp.tanh(0.7978845608028654 * (x + 0.044715 * (x * x * x))))


def _lru_group_scan(a_val, u_val, rev):
    n_grp = LRU_CH // 8
    a_val = a_val.reshape(n_grp, 8, LRU_TILE)
    u_val = u_val.reshape(n_grp, 8, LRU_TILE)
    sub = lax.broadcasted_iota(jnp.int32, (n_grp, 8, LRU_TILE), 1)
    for s in (1, 2, 4):
        shift = 8 - s if rev else s
        valid = (sub < 8 - s) if rev else (sub >= s)
        a_sh = pltpu.roll(a_val, shift, axis=1)
        u_sh = pltpu.roll(u_val, shift, axis=1)
        u_val = jnp.where(valid, a_val * u_sh + u_val, u_val)
        a_val = jnp.where(valid, a_val * a_sh, a_val)
    return a_val.reshape(LRU_CH, LRU_TILE), u_val.reshape(LRU_CH, LRU_TILE)


def _lru_carry(a_cum, h_loc, hc, rev):
    n_grp = LRU_CH // 8
    pieces = [None] * n_grp
    for j in (range(n_grp - 1, -1, -1) if rev else range(n_grp)):
        h = h_loc[8 * j:8 * j + 8] + a_cum[8 * j:8 * j + 8] * hc
        pieces[j] = h
        hc = h[0:1] if rev else h[7:8]
    return jnp.concatenate(pieces, axis=0), hc


def _lru_kernel(seq_len, n_rows, lx_ref, lg_ref, cw_ref, cb_ref, wg_ref, bg_ref, lam_ref, h0_ref,
                y_ref, hfin_ref, ac_scr, hl_scr):
    n_ch = n_rows // LRU_CH
    hfin_ref[...] = jnp.zeros_like(hfin_ref)
    cw = cw_ref[...]
    cb = cb_ref[...]

    def seq_edges(start):
        return (start % seq_len) == 0, ((start + LRU_CH) % seq_len) == 0

    def forward_chunk(ci, hc):
        start = pl.multiple_of(ci * LRU_CH, LRU_CH)
        at_seq_start, at_seq_end = seq_edges(start)
        hr = LRU_HALO
        prev = lx_ref[pl.ds(pl.multiple_of(jnp.maximum(start - hr, 0), hr), hr), :].astype(F32)
        nxt = lx_ref[pl.ds(pl.multiple_of(jnp.minimum(start + LRU_CH, n_rows - hr), hr), hr), :].astype(F32)
        prev = jnp.where(at_seq_start, 0.0, prev)
        nxt = jnp.where(at_seq_end, 0.0, nxt)
        cur = lx_ref[pl.ds(start, LRU_CH), :].astype(F32)
        ext = jnp.concatenate([prev, cur, nxt], axis=0)
        n_ext = LRU_CH + 2 * hr
        xm2 = pltpu.roll(ext, 2, axis=0)[hr:hr + LRU_CH]
        xm1 = pltpu.roll(ext, 1, axis=0)[hr:hr + LRU_CH]
        xp1 = pltpu.roll(ext, n_ext - 1, axis=0)[hr:hr + LRU_CH]
        xb = cb + xm2 * cw[0:1] + xm1 * cw[1:2] + cur * cw[2:3] + xp1 * cw[3:4]
        xb_bf = xb.astype(BF16)
        rows = pl.ds(start, LRU_CH)
        for d in range(2):
            pre = _dot(xb_bf, wg_ref[d]) + bg_ref[d]
            r_g = _sigmoid(pre[:, :LRU_TILE])
            i_g = _sigmoid(pre[:, LRU_TILE:])
            log_a = (LRU_C * r_g) * (-_softplus(-lam_ref[d]))
            a_val = jnp.exp(log_a)
            th = jnp.tanh(log_a)
            q = -2.0 * th / (1.0 - th)
            root = jnp.where(q > 0.0, q * lax.rsqrt(q), 0.0)
            a_cum, h_loc = _lru_group_scan(a_val, root * i_g * xb, d == 1)
            if d == 0:
                hc = jnp.where(at_seq_start, h0_ref[0], hc)
                h_all, hc = _lru_carry(a_cum, h_loc, hc, False)
                y_ref[rows, :] = h_all
            else:
                ac_scr[rows, :] = a_cum
                hl_scr[rows, :] = h_loc

        @pl.when(at_seq_end)
        def _():
            hfin_ref[0, pl.ds(start // seq_len, 1), :] = hc

        return hc

    def backward_chunk(ci, hc):
        start = pl.multiple_of((n_ch - 1 - ci) * LRU_CH, LRU_CH)
        at_seq_start, at_seq_end = seq_edges(start)
        rows = pl.ds(start, LRU_CH)
        hc = jnp.where(at_seq_end, h0_ref[1], hc)
        h_all, hc = _lru_carry(ac_scr[rows, :], hl_scr[rows, :], hc, True)
        y_ref[rows, :] = (y_ref[rows, :] + h_all) * _gelu_tanh(lg_ref[rows, :].astype(F32))

        @pl.when(at_seq_start)
        def _():
            hfin_ref[1, pl.ds(start // seq_len, 1), :] = hc

        return hc

    zero = jnp.zeros((1, LRU_TILE), F32)
    lax.fori_loop(0, n_ch, forward_chunk, zero)
    lax.fori_loop(0, n_ch, backward_chunk, zero)


def _lru_scan(p, n_seq, seq_len, h0, wts):
    t = p.shape[0]
    n_rows = max(seq_len, 8 * LRU_CH)
    nb = t // n_rows
    nt = D_RNN // LRU_TILE
    chain = h0 is not None
    if not chain:
        h0 = jnp.zeros((1, 2, 1, D_RNN), F32)
        h0_spec = pl.BlockSpec((None, 2, 1, LRU_TILE), lambda i, j: (0, 0, 0, j))
    else:
        h0 = h0.reshape(n_seq, 2, 1, D_RNN)
        h0_spec = pl.BlockSpec((None, 2, 1, LRU_TILE), lambda i, j: (i, 0, 0, j))
    lg0 = COL_LG // LRU_TILE
    y, hfin = pl.pallas_call(
        functools.partial(_lru_kernel, seq_len, n_rows),
        grid=(nb, nt),
        in_specs=[
            pl.BlockSpec((n_rows, LRU_TILE), lambda i, j: (i, j)),
            pl.BlockSpec((n_rows, LRU_TILE), lambda i, j: (i, lg0 + j)),
            pl.BlockSpec((CONV_W, LRU_TILE), lambda i, j: (0, j)),
            pl.BlockSpec((1, LRU_TILE), lambda i, j: (0, j)),
            pl.BlockSpec((2, None, LRU_TILE, 2 * LRU_TILE), lambda i, j: (0, j, 0, 0)),
            pl.BlockSpec((2, None, 1, 2 * LRU_TILE), lambda i, j: (0, j, 0, 0)),
            pl.BlockSpec((2, 1, LRU_TILE), lambda i, j: (0, 0, j)),
            h0_spec,
        ],
        out_specs=[
            pl.BlockSpec((n_rows, LRU_TILE), lambda i, j: (i, j)),
            pl.BlockSpec((2, None, 8, LRU_TILE), lambda i, j: (0, i, 0, j)),
        ],
        out_shape=[jax.ShapeDtypeStruct((t, D_RNN), F32), jax.ShapeDtypeStruct((2, nb, 8, D_RNN), F32)],
        scratch_shapes=[pltpu.VMEM((n_rows, LRU_TILE), F32), pltpu.VMEM((n_rows, LRU_TILE), F32)],
        compiler_params=_cparams(("parallel", "parallel")),
        name="lru_scan",
    )(p, p, wts["conv_w"], wts["conv_b"], wts["lru_wg"], wts["lru_bg"], wts["lru_lam"], h0)
    return y, hfin


def _lru_gate_weights(wa, ba, wx, bx):
    nt = D_RNN // LRU_TILE
    per = LRU_TILE // LRU_BLOCK
    eye = jnp.eye(per, dtype=F32)

    def tiles(w):
        w = w.reshape(2, nt, per, LRU_BLOCK, LRU_BLOCK)
        bd = jnp.einsum("dtaij,ab->dtaibj", w, eye)
        return bd.reshape(2, nt, LRU_TILE, LRU_TILE)

    wg = jnp.concatenate([tiles(wa), tiles(wx)], axis=-1).astype(BF16)
    bg = jnp.concatenate([ba.reshape(2, nt, 1, LRU_TILE), bx.reshape(2, nt, 1, LRU_TILE)], axis=-1)
    return wg, bg


ROUTER_LANES = 128
ROUTE_TILE = 256
PACK_W = 256


def _pack_bf16_pairs(x):
    bits = pltpu.bitcast(x.astype(BF16).astype(F32), jnp.uint32)
    hi_mask = jnp.uint32(0xFFFF0000)

    def pack(hi, lo):
        return (hi & hi_mask) | (lo >> 16)

    return (pack(bits[:, 0:PACK_W], bits[:, PACK_W:2 * PACK_W]),
            pack(bits[:, 2 * PACK_W:3 * PACK_W], bits[:, 3 * PACK_W:4 * PACK_W]))


def _unpack_bf16_pairs(pa, pb):
    hi_mask = jnp.uint32(0xFFFF0000)
    parts = []
    for p in (pa, pb):
        parts.append(pltpu.bitcast(p & hi_mask, F32))
        parts.append(pltpu.bitcast(p << 16, F32))
    return jnp.concatenate(parts, axis=1)


def _head_sum(x, ones_bd):
    m, n_g = x.shape[0], x.shape[1] // HG
    stacked = jnp.concatenate([x[:, g * HG:(g + 1) * HG] for g in range(n_g)], axis=0)
    sums = _dot_exact_rhs(stacked, ones_bd)
    return jnp.concatenate([sums[g * m:(g + 1) * m] for g in range(n_g)], axis=1)


def _merge_kernel(o0_ref, o1_ref, rkv_ref, sm_ref, ga_ref, gb_ref, yb_ref, x_ref, mod_ref,
                  aup_ref, a0_ref, ka_ref, rk_ref, gup_ref, lng_ref, lnb_ref, proj_ref, bm_ref,
                  wout_ref, g2_ref, rw_ref, rb_ref, x1_ref, ha_ref, hb_ref, route_ref, cnt_ref):
    ones_bd = _head_ones(HG)
    o = o0_ref[...] + o1_ref[...]
    mu = _head_sum(o, ones_bd) * (1.0 / HEAD_DIM)
    oc = o - mu
    var = _head_sum(oc * oc, ones_bd) * (1.0 / HEAD_DIM)
    on = oc * lax.rsqrt(var + GN_EPS)
    r = rkv_ref[0].astype(F32)
    k = rkv_ref[1].astype(F32)
    v = rkv_ref[2].astype(F32)
    sm = sm_ref[...]
    ad = sm[:, 64:128]
    gd = sm[:, 128:256]
    ka = ka_ref[...]
    mix = (rk_ref[0] + rk_ref[1]) * (1.0 - 0.5 * ka)
    for d in range(2):
        t_d = jnp.tanh(0.5 * (a0_ref[d] + _dot3(ad, aup_ref[d])))
        mix = mix + (0.5 * ka * rk_ref[d]) * t_d
    bonus = _head_sum(r * (k * mix), ones_bd) * v
    g = _dot(_sigmoid(gd).astype(BF16), gup_ref[...])
    y_a = (on * lng_ref[...] + lnb_ref[...] + bonus) * g
    y_b = _dot(yb_ref[...].astype(BF16), proj_ref[...])
    bm = bm_ref[...]
    gate_a = _sigmoid(ga_ref[...].astype(F32) + bm[:, :D_MODEL])
    gate_b = _sigmoid(gb_ref[...].astype(F32) + bm[:, D_MODEL:])
    y = gate_a * y_a + gate_b * y_b
    m = mod_ref[...]
    gt1 = m[:, 2 * D_MODEL:3 * D_MODEL]
    sh2 = m[:, 3 * D_MODEL:4 * D_MODEL]
    sc2 = m[:, 4 * D_MODEL:5 * D_MODEL]
    x1 = x_ref[...] + gt1 * _dot(y.astype(BF16), wout_ref[...])
    x1_ref[...] = x1
    ms = jnp.mean(x1 * x1, axis=-1, keepdims=True)
    h2 = (x1 * lax.rsqrt(ms + RMS_EPS) * g2_ref[...]) * (1.0 + sc2) + sh2
    ha_ref[...], hb_ref[...] = _pack_bf16_pairs(h2)

    logits = _dot3(h2, rw_ref[...]) + rb_ref[...]
    lane = lax.broadcasted_iota(jnp.int32, logits.shape, 1)
    work = logits
    vals, sels, firsts = [], [], []
    for _ in range(TOP_K):
        mx = jnp.max(work, axis=-1, keepdims=True)
        first = jnp.min(jnp.where(work == mx, lane, ROUTER_LANES), axis=-1, keepdims=True)
        sel = lane == first
        vals.append(mx)
        sels.append(sel)
        firsts.append(first)
        work = jnp.where(sel, -jnp.inf, work)
    es = [jnp.exp(val - vals[0]) for val in vals]
    inv = 1.0 / (es[0] + es[1] + es[2] + es[3])
    mask = jnp.where(sels[0] | sels[1] | sels[2] | sels[3], 1.0, 0.0)
    tm = mask.shape[0]
    ri = lax.broadcasted_iota(jnp.int32, (tm, tm), 0)
    ci = lax.broadcasted_iota(jnp.int32, (tm, tm), 1)
    earlier = jnp.where(ci < ri, 1.0, 0.0).astype(BF16)
    rank = _dot(earlier, mask.astype(BF16))
    route = jnp.zeros_like(logits)
    for j in range(TOP_K):
        rank_j = jnp.sum(jnp.where(sels[j], rank, 0.0), axis=-1, keepdims=True)
        route = jnp.where(lane == j, firsts[j].astype(F32), route)
        route = jnp.where(lane == TOP_K + j, es[j] * inv, route)
        route = jnp.where(lane == 2 * TOP_K + j, rank_j, route)
    route_ref[...] = route
    cnt_ref[...] = jnp.broadcast_to(jnp.sum(mask, axis=0, keepdims=True), cnt_ref.shape)


def _merge(o0, o1, rkv, small, p, yb, x, mod3, mod_base, rows_per_mod, w):
    t = x.shape[0]
    tm = ROUTE_TILE

    def row(width, col=0):
        return pl.BlockSpec((tm, width), lambda i: (i, col))

    def full(shape):
        nd = len(shape)
        return pl.BlockSpec(shape, lambda i: (0,) * nd)

    return pl.pallas_call(
        _merge_kernel,
        grid=(t // tm,),
        in_specs=[
            row(D_MODEL), row(D_MODEL),
            pl.BlockSpec((3, tm, D_MODEL), lambda i: (0, i, 0)),
            pl.BlockSpec((None, tm, 256), lambda i: (0, i, 0)),
            row(D_MODEL, COL_GA // D_MODEL), row(D_MODEL, COL_GB // D_MODEL),
            row(D_RNN), row(D_MODEL),
            pl.BlockSpec((None, 1, 6 * D_MODEL), lambda i: (mod_base + (i * tm) // rows_per_mod, 0, 0)),
            full((2, 64, D_MODEL)), full((2, 1, D_MODEL)), full((1, D_MODEL)), full((2, 1, D_MODEL)),
            full((128, D_MODEL)), full((1, D_MODEL)), full((1, D_MODEL)), full((D_RNN, D_MODEL)),
            full((1, 2 * D_MODEL)), full((D_MODEL, D_MODEL)), full((1, D_MODEL)),
            full((D_MODEL, ROUTER_LANES)), full((1, ROUTER_LANES)),
        ],
        out_specs=[row(D_MODEL), row(PACK_W), row(PACK_W), row(ROUTER_LANES),
                   pl.BlockSpec((None, 8, ROUTER_LANES), lambda i: (i, 0, 0))],
        out_shape=[jax.ShapeDtypeStruct((t, D_MODEL), F32),
                   jax.ShapeDtypeStruct((t, PACK_W), jnp.uint32), jax.ShapeDtypeStruct((t, PACK_W), jnp.uint32),
                   jax.ShapeDtypeStruct((t, ROUTER_LANES), F32),
                   jax.ShapeDtypeStruct((t // tm, 8, ROUTER_LANES), F32)],
        compiler_params=_cparams(("parallel",)),
        name="merge",
    )(o0, o1, rkv, small, p, p, yb, x, mod3,
      w["a_up"], w["a0"], w["k_a"], w["r_k"], w["g_up"], w["ln_g"], w["ln_b"], w["lru_proj"],
      w["b_merge"], w["w_out"], w["norm_ffn_g"], w["router_w"], w["router_b"])


EXPERT_TILE = 528
SC_WINDOW = 128


def _sc_mesh():
    return plsc.VectorSubcoreMesh(core_axis_name="core", subcore_axis_name="subcore")


def _sc_scatter_rows(x, pos_flat, n_out):
    t, width = x.shape
    n_idx = pos_flat.shape[1]
    n_src = t // SC_WINDOW

    @pl.kernel(out_type=jax.ShapeDtypeStruct((n_out, width), x.dtype), mesh=_sc_mesh(), scratch_types=[])
    def scatter(x_hbm, i_hbm, o_hbm):
        def body(x_vmem, i_vmem):
            pltpu.sync_copy(x_vmem, o_hbm.at[i_vmem.at[0]])

        pltpu.emit_pipeline(
            body,
            grid=(n_idx // SC_WINDOW,),
            in_specs=[pl.BlockSpec((SC_WINDOW, width), index_map=lambda i: (i % n_src, 0)),
                      pl.BlockSpec((1, SC_WINDOW), index_map=lambda i: (0, i))],
            out_specs=[],
            core_axis_name=("core", "subcore"),
            dimension_semantics=(pltpu.PARALLEL,),
        )(x_hbm, i_hbm)

    return scatter(x, pos_flat)


def _sc_gather_rows(y, pos_flat):
    n_idx = pos_flat.shape[1]
    width = y.shape[1]

    @pl.kernel(out_type=jax.ShapeDtypeStruct((n_idx, width), y.dtype), mesh=_sc_mesh(), scratch_types=[])
    def gather(y_hbm, i_hbm, o_hbm):
        def body(i_vmem, o_vmem):
            pltpu.sync_copy(y_hbm.at[i_vmem.at[0]], o_vmem)

        pltpu.emit_pipeline(
            body,
            grid=(n_idx // SC_WINDOW,),
            in_specs=[pl.BlockSpec((1, SC_WINDOW), index_map=lambda i: (0, i))],
            out_specs=[pl.BlockSpec((SC_WINDOW, width), index_map=lambda i: (i, 0))],
            core_axis_name=("core", "subcore"),
            dimension_semantics=(pltpu.PARALLEL,),
        )(i_hbm, o_hbm)

    return gather(y, pos_flat)


def _expert_kernel(te_ref, nt_ref, xa_ref, xb_ref, wgu_ref, bgu_ref, wd_ref, bd_ref, ya_ref, yb_ref,
                   wgu_bf, wd_bf):
    i = pl.program_id(0)
    prev = te_ref[jnp.maximum(i - 1, 0)]
    fresh = (i == 0) | (te_ref[i] != prev)

    @pl.when(fresh)
    def _():
        wgu_bf[...] = wgu_ref[...].astype(BF16)
        wd_bf[...] = wd_ref[...].astype(BF16)

    @pl.when(i < nt_ref[0])
    def _():
        d_exp = wd_bf.shape[0]
        bgu = bgu_ref[...]
        x = _unpack_bf16_pairs(xa_ref[...], xb_ref[...]).astype(BF16)
        gate = _dot(x, wgu_bf[:, :d_exp]) + bgu[:, :d_exp]
        up = _dot(x, wgu_bf[:, d_exp:]) + bgu[:, d_exp:]
        gate = jnp.minimum(gate, SWIGLU_LIMIT)
        up = jnp.clip(up, -SWIGLU_LIMIT, SWIGLU_LIMIT)
        act = gate * _sigmoid(SWIGLU_ALPHA * gate) * (up + 1.0)
        y = _dot(act.astype(BF16), wd_bf[...]) + bd_ref[...]
        ya_ref[...], yb_ref[...] = _pack_bf16_pairs(y)


def _experts(xs_a, xs_b, tile_expert, n_tiles_used, w):
    n_rows = xs_a.shape[0]
    n_tiles = n_rows // EXPERT_TILE
    n_exp, _, two_d = w["exp_w_gu"].shape
    d_exp = two_d // 2

    def rows(i, te, nt):
        return (jnp.minimum(i, nt[0] - 1), 0)

    def by_expert(i, te, nt):
        return (te[i], 0, 0)

    grid_spec = pltpu.PrefetchScalarGridSpec(
        num_scalar_prefetch=2,
        grid=(n_tiles,),
        in_specs=[
            pl.BlockSpec((EXPERT_TILE, PACK_W), rows),
            pl.BlockSpec((EXPERT_TILE, PACK_W), rows),
            pl.BlockSpec((None, D_MODEL, two_d), by_expert),
            pl.BlockSpec((None, 1, two_d), by_expert),
            pl.BlockSpec((None, d_exp, D_MODEL), by_expert),
            pl.BlockSpec((None, 1, D_MODEL), by_expert),
        ],
        out_specs=[pl.BlockSpec((EXPERT_TILE, PACK_W), rows), pl.BlockSpec((EXPERT_TILE, PACK_W), rows)],
        scratch_shapes=[pltpu.VMEM((D_MODEL, two_d), BF16), pltpu.VMEM((d_exp, D_MODEL), BF16)],
    )
    return pl.pallas_call(
        _expert_kernel,
        grid_spec=grid_spec,
        out_shape=[jax.ShapeDtypeStruct((n_rows, PACK_W), jnp.uint32)] * 2,
        compiler_params=_cparams(("arbitrary",)),
        name="experts",
    )(tile_expert, n_tiles_used, xs_a, xs_b, w["exp_w_gu"], w["exp_b_gu"], w["exp_w_down"], w["exp_b_down"])


def _combine_kernel(ya_ref, yb_ref, route_ref, x1_ref, mod_ref, fg_ref, o_ref):
    route = route_ref[...]
    lane = lax.broadcasted_iota(jnp.int32, route.shape, 1)
    acc = jnp.zeros(x1_ref.shape, F32)
    for j in range(TOP_K):
        w_j = jnp.sum(jnp.where(lane == TOP_K + j, route, 0.0), axis=-1, keepdims=True)
        acc = acc + w_j * _unpack_bf16_pairs(ya_ref[j], yb_ref[j])
    gt2 = mod_ref[...][:, 5 * D_MODEL:6 * D_MODEL]
    x2 = x1_ref[...] + gt2 * acc
    ms = jnp.mean(x2 * x2, axis=-1, keepdims=True)
    o_ref[...] = x2 * lax.rsqrt(ms + RMS_EPS) * fg_ref[...]


def _combine(yg_a, yg_b, route, x1, mod3, mod_base, rows_per_mod, w):
    t = x1.shape[0]
    tm = ROUTE_TILE
    return pl.pallas_call(
        _combine_kernel,
        grid=(t // tm,),
        in_specs=[
            pl.BlockSpec((TOP_K, tm, PACK_W), lambda i: (0, i, 0)),
            pl.BlockSpec((TOP_K, tm, PACK_W), lambda i: (0, i, 0)),
            pl.BlockSpec((tm, ROUTER_LANES), lambda i: (i, 0)),
            pl.BlockSpec((tm, D_MODEL), lambda i: (i, 0)),
            pl.BlockSpec((None, 1, 6 * D_MODEL), lambda i: (mod_base + (i * tm) // rows_per_mod, 0, 0)),
            pl.BlockSpec((1, D_MODEL), lambda i: (0, 0)),
        ],
        out_specs=pl.BlockSpec((tm, D_MODEL), lambda i: (i, 0)),
        out_shape=jax.ShapeDtypeStruct((t, D_MODEL), F32),
        compiler_params=_cparams(("parallel",)),
        name="combine",
    )(yg_a, yg_b, route, x1, mod3, w["final_norm_g"])


def _routing_tables(route, cnt, n_exp):
    t = route.shape[0]
    e4 = route[:, 0:TOP_K].astype(jnp.int32)
    r4 = route[:, 2 * TOP_K:3 * TOP_K].astype(jnp.int32)
    cnt = cnt[:, 0, :n_exp].astype(jnp.int32)
    tile_off = jnp.cumsum(cnt, axis=0) - cnt
    n_e = jnp.sum(cnt, axis=0)
    region = (n_e + EXPERT_TILE - 1) // EXPERT_TILE * EXPERT_TILE
    region_end = jnp.cumsum(region)
    base = region_end - region
    seg_start = jnp.repeat(base[None, :] + tile_off, ROUTE_TILE, axis=0)
    onehot = e4[:, :, None] == jnp.arange(n_exp, dtype=jnp.int32)[None, None, :]
    pos4 = jnp.sum(jnp.where(onehot, seg_start[:, None, :], 0), axis=-1) + r4
    n_tiles_max = (TOP_K * t + n_exp * (EXPERT_TILE - 1)) // EXPERT_TILE
    tile_start = jnp.arange(n_tiles_max, dtype=jnp.int32) * EXPERT_TILE
    regions_before = jnp.sum(tile_start[:, None] >= region_end[None, :], axis=1)
    tile_expert = jnp.minimum(regions_before, n_exp - 1)
    n_used = (region_end[-1] // EXPERT_TILE).reshape(1)
    return pos4.T, tile_expert.astype(jnp.int32), n_used.astype(jnp.int32), n_tiles_max


_N_SHIFT = 3328
_COL_RANGES = (
    (_N_SHIFT, _N_SHIFT + 2 * D_RNN),
    (0, 3 * D_MODEL),
    (_N_SHIFT + 2 * D_RNN, _N_SHIFT + 2 * D_RNN + 2 * D_MODEL),
    (3 * D_MODEL, _N_SHIFT),
)


def _permute_cols(a):
    return jnp.concatenate([a[..., lo:hi] for lo, hi in _COL_RANGES], axis=-1)


def _stream(x, n_seq, seq_len, mod3, mod_base, rows_per_mod, s_wkv0, s_lru0, grid_mode, w):
    t = n_seq * seq_len
    x2 = x.reshape(t, D_MODEL)
    p = _in_projection(x2, mod3, mod_base, rows_per_mod, w["norm_mix_g"], w["w_in"])
    rkv = _token_shift(p, w["mu"], COL_R, D_MODEL, (D_MODEL,), 3, seq_len, grid_mode, BF16)
    small = _token_shift(p, w["mu"], COL_SMALL, 256, (64, 64, 128), 1, seq_len, grid_mode, F32)
    o0, o1, s_wkv = _wkv_scan(rkv, small, n_seq, seq_len, s_wkv0, w)
    yb, s_lru = _lru_scan(p, n_seq, seq_len, s_lru0, w)
    x1, h_a, h_b, route, cnt = _merge(o0, o1, rkv, small, p, yb, x2, mod3, mod_base, rows_per_mod, w)
    return dict(x1=x1, h_a=h_a, h_b=h_b, route=route, cnt=cnt, s_wkv=s_wkv, s_lru=s_lru)


def _moe_and_norm(streams, mods, n_exp, w):
    h_a = jnp.concatenate([s["h_a"] for s in streams], axis=0)
    h_b = jnp.concatenate([s["h_b"] for s in streams], axis=0)
    route = jnp.concatenate([s["route"] for s in streams], axis=0)
    cnt = jnp.concatenate([s["cnt"] for s in streams], axis=0)
    pos, tile_expert, n_used, n_tiles_max = _routing_tables(route, cnt, n_exp)
    n_rows = n_tiles_max * EXPERT_TILE
    pos_flat = pos.reshape(1, -1)
    xs_a = _sc_scatter_rows(h_a, pos_flat, n_rows)
    xs_b = _sc_scatter_rows(h_b, pos_flat, n_rows)
    ys_a, ys_b = _experts(xs_a, xs_b, tile_expert, n_used, w)
    outs, row0 = [], 0
    for s, (mod3, mod_base, rows_per_mod) in zip(streams, mods):
        t = s["x1"].shape[0]
        pos_s = pos[:, row0:row0 + t].reshape(1, TOP_K * t)
        yg_a = _sc_gather_rows(ys_a, pos_s).reshape(TOP_K, t, PACK_W)
        yg_b = _sc_gather_rows(ys_b, pos_s).reshape(TOP_K, t, PACK_W)
        outs.append(_combine(yg_a, yg_b, s["route"], s["x1"], mod3, mod_base, rows_per_mod, w))
        row0 += t
    return outs


def kernel(x_prompt, x_sample, state_wkv, state_lru, c, c_ctx, norm_mix_g, norm_ffn_g, w_ada, b_ada, w_in, shift_mu, wkv_k_k, wkv_k_a, wkv_r_k, wkv_w0, wkv_w_up, wkv_a0, wkv_a_up, wkv_g_up, wkv_ln_g, wkv_ln_b, lru_conv_w, lru_conv_b, lru_wa, lru_ba, lru_wx, lru_bx, lru_lambda, lru_proj, b_merge, w_out, router_w, router_b, exp_w_gu, exp_b_gu, exp_w_down, exp_b_down, final_norm_g):
    n_ctx, seq, _ = x_prompt.shape
    n_dec, dec_seq, _ = x_sample.shape
    assert w_in.shape[0] == 1, "single trunk layer: the final norm is fused into the MoE kernel"
    assert 1 + n_dec <= 8 and dec_seq % GRID_W == 0
    l = 0
    n_exp = router_w.shape[-1]
    cond8 = jnp.concatenate([c_ctx[None, :], c, jnp.zeros((8 - 1 - n_dec, D_MODEL), F32)], axis=0)
    mod3 = _modulation(cond8, w_ada[l], b_ada[l]).reshape(8, 1, 6 * D_MODEL)
    mu_full = jnp.concatenate([shift_mu[l], jnp.zeros((N_IN - _N_SHIFT,), F32)])
    lru_wg, lru_bg = _lru_gate_weights(lru_wa[l], lru_ba[l], lru_wx[l], lru_bx[l])
    router_w_pad = jnp.zeros((D_MODEL, ROUTER_LANES), F32).at[:, :n_exp].set(router_w[l])
    router_b_pad = jnp.full((1, ROUTER_LANES), -1e30, F32).at[0, :n_exp].set(router_b[l])
    w = dict(
        norm_mix_g=norm_mix_g[l],
        w_in=_permute_cols(w_in[l]).astype(BF16),
        mu=_permute_cols(mu_full).reshape(1, N_IN),
        k_k=wkv_k_k[l].reshape(1, D_MODEL), k_a=wkv_k_a[l].reshape(1, D_MODEL),
        w0=wkv_w0[l].reshape(2, 1, D_MODEL), w_up=wkv_w_up[l],
        a0=wkv_a0[l].reshape(2, 1, D_MODEL), a_up=wkv_a_up[l],
        r_k=wkv_r_k[l].reshape(2, 1, D_MODEL), g_up=wkv_g_up[l].astype(BF16),
        ln_g=wkv_ln_g[l].reshape(1, D_MODEL), ln_b=wkv_ln_b[l].reshape(1, D_MODEL),
        conv_w=lru_conv_w[l], conv_b=lru_conv_b[l].reshape(1, D_RNN),
        lru_wg=lru_wg, lru_bg=lru_bg, lru_lam=lru_lambda[l].reshape(2, 1, D_RNN),
        lru_proj=lru_proj[l].astype(BF16), b_merge=b_merge[l].reshape(1, 2 * D_MODEL),
        w_out=w_out[l].astype(BF16), norm_ffn_g=norm_ffn_g[l].reshape(1, D_MODEL),
        router_w=router_w_pad, router_b=router_b_pad,
        exp_w_gu=exp_w_gu[l], exp_b_gu=exp_b_gu[l].reshape(n_exp, 1, -1),
        exp_w_down=exp_w_down[l], exp_b_down=exp_b_down[l].reshape(n_exp, 1, D_MODEL),
        final_norm_g=final_norm_g.reshape(1, D_MODEL),
    )
    ctx = _stream(x_prompt, n_ctx, seq, mod3, 0, n_ctx * seq, None, None, False, w)
    smp = _stream(x_sample, n_dec, dec_seq, mod3, 1, dec_seq, state_wkv[:, l], state_lru[:, l], True, w)
    y_prompt, y_sample = _moe_and_norm([ctx, smp], [(mod3, 0, n_ctx * seq), (mod3, 1, dec_seq)], n_exp, w)
    new_lru = ctx["s_lru"].reshape(2, n_ctx, D_RNN).transpose(1, 0, 2)
    return (y_prompt.reshape(x_prompt.shape), y_sample.reshape(x_sample.shape),
            ctx["s_wkv"][:, None].astype(state_wkv.dtype), new_lru[:, None].astype(state_lru.dtype))
```

```python
import functools

import numpy as np
import jax
import jax.numpy as jnp
from jax import lax
from jax.experimental import pallas as pl
from jax.experimental.pallas import tpu as pltpu
from jax.experimental.pallas import tpu_sc as plsc

F32 = jnp.float32
BF16 = jnp.bfloat16

D_MODEL = 1024
HEAD_DIM = 64
N_HEADS = 16
GRID_W = 64
D_RNN = 1536
LRU_BLOCK = 96
CONV_W = 4
LRU_C = 8.0
TOP_K = 4
SWIGLU_LIMIT = 7.0
SWIGLU_ALPHA = 1.702
RMS_EPS = 1e-6
GN_EPS = 64e-5

COL_LX = 0
COL_LG = 1536
COL_R = 3072
COL_K = 4096
COL_V = 5120
COL_GA = 6144
COL_GB = 7168
COL_SMALL = 8192
N_IN = 8448

HG = 256
WKV_CHUNK = 64
WKV_GROUPS_PER_STEP = 4
VMEM_LIMIT = 56 * 1024 * 1024


def _cparams(sem, vmem=VMEM_LIMIT):
    return pltpu.CompilerParams(dimension_semantics=sem, vmem_limit_bytes=vmem)


def _split2(x):
    hi = x.astype(BF16)
    lo = (x - hi.astype(F32)).astype(BF16)
    return hi, lo


def _dot(a, b):
    return jnp.dot(a, b, preferred_element_type=F32)


def _dot3(a, b):
    ah, al = _split2(a)
    bh, bl = _split2(b)
    return _dot(ah, bh) + (_dot(al, bh) + _dot(ah, bl))


def _stack3_rhs(b):
    bh, bl = _split2(b)
    return jnp.concatenate([bh, bh, bl], axis=-2)


def _dot3_stacked(a, b3):
    ah, al = _split2(a)
    return _dot(jnp.concatenate([ah, al, ah], axis=1), b3)


def _dot_exact_rhs(a, b_bf16):
    ah, al = _split2(a)
    m = a.shape[0]
    both = _dot(jnp.concatenate([ah, al], axis=0), b_bf16)
    return both[:m] + both[m:]


def _head_ones(n):
    r = lax.broadcasted_iota(jnp.int32, (n, n), 0) // HEAD_DIM
    c = lax.broadcasted_iota(jnp.int32, (n, n), 1) // HEAD_DIM
    return jnp.where(r == c, 1.0, 0.0).astype(BF16)


def _sigmoid(x):
    return 0.5 * jnp.tanh(0.5 * x) + 0.5


def _softplus(x):
    return jnp.maximum(x, 0.0) + jnp.log(1.0 + jnp.exp(-jnp.abs(x)))


def _mod_kernel(c_ref, w_ref, b_ref, o_ref):
    c = c_ref[...]
    s = c * _sigmoid(c)
    o_ref[...] = _dot3(s, w_ref[...]) + b_ref[...]


def _modulation(cond8, w_ada, b_ada):
    n = w_ada.shape[1]
    tn = 1024
    return pl.pallas_call(
        _mod_kernel,
        grid=(n // tn,),
        in_specs=[
            pl.BlockSpec((8, D_MODEL), lambda j: (0, 0)),
            pl.BlockSpec((D_MODEL, tn), lambda j: (0, j)),
            pl.BlockSpec((1, tn), lambda j: (0, j)),
        ],
        out_specs=pl.BlockSpec((8, tn), lambda j: (0, j)),
        out_shape=jax.ShapeDtypeStruct((8, n), F32),
        compiler_params=_cparams(("parallel",)),
        name="modulation",
    )(cond8, w_ada, b_ada.reshape(1, n))


def _inproj_kernel(x_ref, mod_ref, g_ref, w_ref, o_ref, h_ref):
    @pl.when(pl.program_id(1) == 0)
    def _():
        x = x_ref[...]
        ms = jnp.mean(x * x, axis=-1, keepdims=True)
        y = x * lax.rsqrt(ms + RMS_EPS) * g_ref[...]
        m = mod_ref[...]
        sh1 = m[:, 0:D_MODEL]
        sc1 = m[:, D_MODEL:2 * D_MODEL]
        h_ref[...] = (y * (1.0 + sc1) + sh1).astype(BF16)

    o_ref[...] = _dot(h_ref[...], w_ref[...]).astype(o_ref.dtype)


def _in_projection(x, mod3, mod_base, rows_per_mod, g, w_bf16):
    t = x.shape[0]
    tm, tn = 2048, 768
    n = w_bf16.shape[1]
    return pl.pallas_call(
        _inproj_kernel,
        grid=(t // tm, n // tn),
        in_specs=[
            pl.BlockSpec((tm, D_MODEL), lambda i, j: (i, 0)),
            pl.BlockSpec((None, 1, 6 * D_MODEL), lambda i, j: (mod_base + (i * tm) // rows_per_mod, 0, 0)),
            pl.BlockSpec((1, D_MODEL), lambda i, j: (0, 0)),
            pl.BlockSpec((D_MODEL, tn), lambda i, j: (0, j)),
        ],
        out_specs=pl.BlockSpec((tm, tn), lambda i, j: (i, j)),
        out_shape=jax.ShapeDtypeStruct((t, n), BF16),
        scratch_shapes=[pltpu.VMEM((tm, D_MODEL), BF16)],
        compiler_params=_cparams(("parallel", "arbitrary")),
        name="in_projection",
    )(x, mod3, g.reshape(1, D_MODEL), w_bf16)


def _shift_seq_kernel(seq_len, ranges, p_ref, mu_ref, dir_ref, o_ref):
    x = p_ref[...].astype(F32)
    tr = x.shape[0]

    def prev_of(z):
        pos = lax.broadcasted_iota(jnp.int32, z.shape, 0) % seq_len
        return jnp.where(pos == 0, 0.0, pltpu.roll(z, 1, axis=0))

    def next_of(z):
        pos = lax.broadcasted_iota(jnp.int32, z.shape, 0) % seq_len
        return jnp.where(pos == seq_len - 1, 0.0, pltpu.roll(z, tr - 1, axis=0))

    if ranges is None:
        shifted = jnp.where(dir_ref[...] == 0, prev_of(x), next_of(x))
    else:
        shifted = jnp.concatenate([(prev_of if d == 0 else next_of)(x[:, lo:hi]) for lo, hi, d in ranges], axis=1)
    o_ref[...] = (x + mu_ref[...] * (shifted - x)).astype(o_ref.dtype)


def _shift_grid_kernel(seq_len, ranges, p_ref, hp_ref, hn_ref, mu_ref, dir_ref, o_ref):
    x = p_ref[...].astype(F32)
    tr = x.shape[0]
    i = pl.program_id(0)
    tiles_per_seq = seq_len // tr
    first = (i % tiles_per_seq) == 0
    last = (i % tiles_per_seq) == tiles_per_seq - 1

    def left_of(z, lo, hi):
        col = lax.broadcasted_iota(jnp.int32, z.shape, 0) % GRID_W
        return jnp.where(col == 0, 0.0, pltpu.roll(z, 1, axis=0))

    def right_of(z, lo, hi):
        col = lax.broadcasted_iota(jnp.int32, z.shape, 0) % GRID_W
        return jnp.where(col == GRID_W - 1, 0.0, pltpu.roll(z, tr - 1, axis=0))

    def up_of(z, lo, hi):
        hp = jnp.where(first, 0.0, hp_ref[:, lo:hi].astype(F32))
        return jnp.concatenate([hp, z[:tr - GRID_W]], axis=0)

    def down_of(z, lo, hi):
        hn = jnp.where(last, 0.0, hn_ref[:, lo:hi].astype(F32))
        return jnp.concatenate([z[GRID_W:], hn], axis=0)

    fns = (left_of, right_of, up_of, down_of)
    if ranges is None:
        w = x.shape[1]
        d = dir_ref[...]
        shifted = jnp.where(d == 0, left_of(x, 0, w), jnp.where(d == 1, right_of(x, 0, w),
                            jnp.where(d == 2, up_of(x, 0, w), down_of(x, 0, w))))
    else:
        shifted = jnp.concatenate([fns[d](x[:, lo:hi], lo, hi) for lo, hi, d in ranges], axis=1)
    o_ref[...] = (x + mu_ref[...] * (shifted - x)).astype(o_ref.dtype)


def _dir_ranges(widths, n_dirs):
    out, base = [], 0
    for w in widths:
        q = w // n_dirs
        if q % 128:
            return None
        out += [(base + d * q, base + (d + 1) * q, d) for d in range(n_dirs)]
        base += w
    return tuple(out)


def _dir_codes(widths, n_dirs):
    codes = []
    for w in widths:
        q = w // n_dirs
        codes.append(np.repeat(np.arange(n_dirs, dtype=np.int32), q))
    return np.concatenate(codes)[None, :]


def _token_shift(p, mu_perm, col0, width, group_widths, n_groups, seq_len, grid_mode, out_dtype):
    t = p.shape[0]
    tr = 1024
    cb0 = col0 // width
    n_dirs = 4 if grid_mode else 2
    dirs = jnp.asarray(_dir_codes(group_widths, n_dirs))
    ranges = _dir_ranges(group_widths, n_dirs)
    main = pl.BlockSpec((tr, width), lambda i, g: (i, cb0 + g))
    mu_spec = pl.BlockSpec((1, width), lambda i, g: (0, cb0 + g))
    dir_spec = pl.BlockSpec((1, width), lambda i, g: (0, 0))
    out_spec = pl.BlockSpec((None, tr, width), lambda i, g: (g, i, 0))
    out_shape = jax.ShapeDtypeStruct((n_groups, t, width), out_dtype)
    if not grid_mode:
        return pl.pallas_call(
            functools.partial(_shift_seq_kernel, seq_len, ranges),
            grid=(t // tr, n_groups),
            in_specs=[main, mu_spec, dir_spec],
            out_specs=out_spec,
            out_shape=out_shape,
            compiler_params=_cparams(("parallel", "parallel")),
            name="token_shift_seq",
        )(p, mu_perm, dirs)
    hb = tr // GRID_W
    n_hb = t // GRID_W
    halo_prev = pl.BlockSpec((GRID_W, width), lambda i, g: (jnp.maximum(i * hb - 1, 0), cb0 + g))
    halo_next = pl.BlockSpec((GRID_W, width), lambda i, g: (jnp.minimum((i + 1) * hb, n_hb - 1), cb0 + g))
    return pl.pallas_call(
        functools.partial(_shift_grid_kernel, seq_len, ranges),
        grid=(t // tr, n_groups),
        in_specs=[main, halo_prev, halo_next, mu_spec, dir_spec],
        out_specs=out_spec,
        out_shape=out_shape,
        compiler_params=_cparams(("parallel", "parallel")),
        name="token_shift_grid",
    )(p, p, p, mu_perm, dirs)


def _bd_rows(x):
    c = x.shape[0]
    xt = jnp.concatenate([x, x, x, x], axis=0)
    rb = lax.broadcasted_iota(jnp.int32, xt.shape, 0) // c
    lb = lax.broadcasted_iota(jnp.int32, xt.shape, 1) // HEAD_DIM
    return jnp.where(rb == lb, xt, jnp.zeros_like(xt))


def _dot_nt(a, b):
    return lax.dot_general(a, b, (((1,), (1,)), ((), ())), preferred_element_type=F32)


def _dot_tn(a, b):
    return lax.dot_general(a, b, (((0,), (0,)), ((), ())), preferred_element_type=F32)


def _wkv_chains(chains, ones_bd):
    c = chains[0]["r"].shape[0]
    ti = lax.broadcasted_iota(jnp.int32, (c, c), 0)
    si = lax.broadcasted_iota(jnp.int32, (c, c), 1)
    tri = {rev: jnp.where((si >= ti) if rev else (si <= ti), 1.0, 0.0).astype(BF16) for rev in (False, True)}
    tri3 = {rev: jnp.concatenate([tri[rev]] * 3, axis=1) for rev in (False, True)}
    t2 = lax.broadcasted_iota(jnp.int32, (c, 4 * c), 0)
    s2 = lax.broadcasted_iota(jnp.int32, (c, 4 * c), 1) % c
    strict = {False: s2 < t2, True: s2 > t2}
    incl = {False: s2 <= t2, True: s2 >= t2}
    eye = jnp.where(s2 == t2, 1.0, 0.0)

    for ch in chains:
        ch["kkr"] = ch["k"] * ch["kk_w"]
    sq = jnp.concatenate([ch["kkr"] * ch["kkr"] for ch in chains], axis=0)
    ss_all = _dot_exact_rhs(sq, ones_bd)
    for n, ch in enumerate(chains):
        ch["ss"] = ss_all[n * c:(n + 1) * c]
    for ch in chains:
        ch["wlin"] = ch["w0"] + _dot3_stacked(jnp.tanh(ch["sm"][:, 0:64]), ch["wup"])
    for ch in chains:
        ch["a"] = _sigmoid(ch["a0"] + _dot3_stacked(ch["sm"][:, 64:128], ch["aup"]))
    for ch in chains:
        logw = -jnp.exp(-_softplus(-ch["wlin"]) - 0.5)
        lh = logw.astype(BF16)
        l1 = logw - lh.astype(F32)
        lm = l1.astype(BF16)
        ll = (l1 - lm.astype(F32)).astype(BF16)
        ch["logw"] = logw
        ch["cum"] = _dot(tri3[ch["rev"]], jnp.concatenate([lh, lm, ll], axis=0))
    for ch in chains:
        rev, cum, k = ch["rev"], ch["cum"], ch["k"]
        kk = ch["kkr"] * lax.rsqrt(jnp.maximum(ch["ss"], 1e-24))
        a = ch["a"]
        kd = k * (1.0 + (a - 1.0) * ch["ka_w"])
        b = kk * a
        ltot = cum[0:1, :] if rev else cum[c - 1:c, :]
        e_neg = jnp.exp(-cum)
        e_tail = jnp.exp(ltot - cum)
        at = -kk * jnp.exp(cum - ch["logw"])
        rt = ch["r"] * jnp.exp(cum)
        ch["ltot"] = ltot
        ch["ar"] = jnp.concatenate([at, rt], axis=0).astype(BF16)
        ch["bt"] = _bd_rows(b * e_neg).astype(BF16)
        ch["kt"] = _bd_rows(kd * e_neg).astype(BF16)
        ch["bk"] = jnp.concatenate([b * e_tail, kd * e_tail], axis=0).astype(BF16)
    for ch in chains:
        ch["m_b"] = _dot_nt(ch["ar"], ch["bt"])
    for ch in chains:
        ch["m_k"] = _dot_nt(ch["ar"], ch["kt"])
    for ch in chains:
        ch["st"] = _dot_nt(ch["ar"], ch["s0"].astype(BF16))
    for ch in chains:
        rev = ch["rev"]
        ch["m_ab"] = jnp.where(strict[rev], ch["m_b"][:c], 0.0)
        ch["n_rb"] = jnp.where(incl[rev], ch["m_b"][c:], 0.0).astype(BF16)
    for ch in chains:
        rev = ch["rev"]
        mk = jnp.concatenate([jnp.where(strict[rev], ch["m_k"][:c], 0.0),
                              jnp.where(incl[rev], ch["m_k"][c:], 0.0)], axis=0).astype(BF16)
        kv = _dot(mk, _bd_rows(ch["v"]).astype(BF16))
        ch["rhs"] = ch["st"][:c] + kv[:c]
        ch["o_v"] = ch["st"][c:] + kv[c:]

    blk = 1
    while blk < c:
        tb = t2 // blk
        sb = s2 // blk
        off = {False: (tb % 2 == 1) & (sb == tb - 1), True: (tb % 2 == 0) & (sb == tb + 1)}
        if blk == 1:
            for ch in chains:
                ch["x"] = eye + jnp.where(off[ch["rev"]], ch["m_ab"], 0.0)
        else:
            for ch in chains:
                m_off = jnp.where(off[ch["rev"]], ch["m_ab"], 0.0)
                ch["p1"] = _dot(ch["x"].astype(BF16), _bd_rows(m_off).astype(BF16))
            for ch in chains:
                ch["x"] = ch["x"] + _dot(ch["p1"].astype(BF16), _bd_rows(ch["x"]).astype(BF16))
        blk *= 2

    for ch in chains:
        ch["u"] = _dot(ch["x"].astype(BF16), _bd_rows(ch["rhs"]).astype(BF16))
    for ch in chains:
        u_bd = _bd_rows(ch["u"]).astype(BF16)
        ch["o"] = ch["o_v"] + _dot(ch["n_rb"], u_bd)
    out = []
    for ch in chains:
        uv = jnp.concatenate([ch["u"], ch["v"]], axis=0).astype(BF16)
        upd = _dot_tn(uv, ch["bk"])
        out.append((ch["o"], ch["s0"] * jnp.exp(ch["ltot"]) + jnp.where(ones_bd > 0, upd, 0.0)))
    return out


def _wkv_kernel(zero_init, nc,
                rf_ref, kf_ref, vf_ref, smf_ref, rb_ref, kb_ref, vb_ref, smb_ref,
                kkw_ref, kaw_ref, w0_ref, wup_ref, a0_ref, aup_ref, s0_ref,
                of_ref, ob_ref, sout_ref, sf_ref, sb_ref):
    ci = pl.program_id(2)
    ones_bd = _head_ones(HG)
    n_g = sf_ref.shape[0]

    @pl.when(ci == 0)
    def _():
        sf_ref[...] = jnp.zeros_like(sf_ref)
        sb_ref[...] = jnp.zeros_like(sb_ref)
        if not zero_init:
            for d, ref in ((0, sf_ref), (1, sb_ref)):
                for h in range(4 * n_g):
                    sl = slice((h % 4) * HEAD_DIM, (h % 4 + 1) * HEAD_DIM)
                    ref[h // 4, sl, sl] = s0_ref[d, h]

    smf = smf_ref[...]
    smb = smb_ref[...]
    chains = []
    for g in range(n_g):
        ln = slice(g * HG, (g + 1) * HG)
        common = dict(kk_w=kkw_ref[:, ln], ka_w=kaw_ref[:, ln])
        chains.append(dict(rev=False, r=rf_ref[:, ln].astype(F32), k=kf_ref[:, ln].astype(F32),
                           v=vf_ref[:, ln].astype(F32), sm=smf,
                           w0=w0_ref[0, :, ln], wup=wup_ref[0, :, ln], a0=a0_ref[0, :, ln],
                           aup=aup_ref[0, :, ln], s0=sf_ref[g], **common))
        chains.append(dict(rev=True, r=rb_ref[:, ln].astype(F32), k=kb_ref[:, ln].astype(F32),
                           v=vb_ref[:, ln].astype(F32), sm=smb,
                           w0=w0_ref[1, :, ln], wup=wup_ref[1, :, ln], a0=a0_ref[1, :, ln],
                           aup=aup_ref[1, :, ln], s0=sb_ref[g], **common))
    results = _wkv_chains(chains, ones_bd)
    for g in range(n_g):
        ln = slice(g * HG, (g + 1) * HG)
        (o_f, s_f), (o_b, s_b) = results[2 * g], results[2 * g + 1]
        of_ref[:, ln] = o_f
        ob_ref[:, ln] = o_b
        sf_ref[g] = s_f
        sb_ref[g] = s_b

    @pl.when(ci == nc - 1)
    def _():
        for d, ref in ((0, sf_ref), (1, sb_ref)):
            for h in range(4 * n_g):
                sl = slice((h % 4) * HEAD_DIM, (h % 4 + 1) * HEAD_DIM)
                sout_ref[d, h] = ref[h // 4, sl, sl]


def _wkv_scan(rkv, small, n_seq, seq_len, s0, wts):
    t = rkv.shape[1]
    c = WKV_CHUNK
    nc = seq_len // c
    gs = WKV_GROUPS_PER_STEP
    wl = gs * HG
    ng = D_MODEL // wl
    zero_init = s0 is None
    if zero_init:
        s0 = jnp.zeros((1, 2, N_HEADS, HEAD_DIM, HEAD_DIM), F32)

    def fwd(which):
        return pl.BlockSpec((None, c, wl), lambda b, g, ci: (which, b * nc + ci, g))

    def bwd(which):
        return pl.BlockSpec((None, c, wl), lambda b, g, ci: (which, b * nc + (nc - 1 - ci), g))

    sm_f = pl.BlockSpec((None, c, 256), lambda b, g, ci: (0, b * nc + ci, 0))
    sm_b = pl.BlockSpec((None, c, 256), lambda b, g, ci: (0, b * nc + (nc - 1 - ci), 0))
    vec = pl.BlockSpec((1, wl), lambda b, g, ci: (0, g))
    vec2 = pl.BlockSpec((2, 1, wl), lambda b, g, ci: (0, 0, g))
    up2 = pl.BlockSpec((2, 3 * 64, wl), lambda b, g, ci: (0, 0, g))
    if zero_init:
        s0_spec = pl.BlockSpec((None, 2, 4 * gs, HEAD_DIM, HEAD_DIM), lambda b, g, ci: (0, 0, g, 0, 0))
    else:
        s0_spec = pl.BlockSpec((None, 2, 4 * gs, HEAD_DIM, HEAD_DIM), lambda b, g, ci: (b, 0, g, 0, 0))
    o_f = pl.BlockSpec((c, wl), lambda b, g, ci: (b * nc + ci, g))
    o_b = pl.BlockSpec((c, wl), lambda b, g, ci: (b * nc + (nc - 1 - ci), g))
    s_out = pl.BlockSpec((None, 2, 4 * gs, HEAD_DIM, HEAD_DIM), lambda b, g, ci: (b, 0, g, 0, 0))
    return pl.pallas_call(
        functools.partial(_wkv_kernel, zero_init, nc),
        grid=(n_seq, ng, nc),
        in_specs=[fwd(0), fwd(1), fwd(2), sm_f, bwd(0), bwd(1), bwd(2), sm_b,
                  vec, vec, vec2, up2, vec2, up2, s0_spec],
        out_specs=[o_f, o_b, s_out],
        out_shape=[jax.ShapeDtypeStruct((t, D_MODEL), F32), jax.ShapeDtypeStruct((t, D_MODEL), F32),
                   jax.ShapeDtypeStruct((n_seq, 2, N_HEADS, HEAD_DIM, HEAD_DIM), F32)],
        scratch_shapes=[pltpu.VMEM((gs, HG, HG), F32), pltpu.VMEM((gs, HG, HG), F32)],
        compiler_params=_cparams(("parallel", "parallel", "arbitrary")),
        name="wkv_scan",
    )(rkv, rkv, rkv, small, rkv, rkv, rkv, small,
      wts["k_k"], wts["k_a"], wts["w0"], wts["w_up"], wts["a0"], wts["a_up"], s0)


LRU_CH = 256
LRU_TILE = 384
LRU_HALO = 16


def _gelu_tanh(x):
    return 0.5 * x * (1.0 + jnp.tanh(0.7978845608028654 * (x + 0.044715 * (x * x * x))))


def _lru_group_scan(a_val, u_val, rev):
    n_grp = LRU_CH // 8
    a_val = a_val.reshape(n_grp, 8, LRU_TILE)
    u_val = u_val.reshape(n_grp, 8, LRU_TILE)
    sub = lax.broadcasted_iota(jnp.int32, (n_grp, 8, LRU_TILE), 1)
    for s in (1, 2, 4):
        shift = 8 - s if rev else s
        valid = (sub < 8 - s) if rev else (sub >= s)
        a_sh = pltpu.roll(a_val, shift, axis=1)
        u_sh = pltpu.roll(u_val, shift, axis=1)
        u_val = jnp.where(valid, a_val * u_sh + u_val, u_val)
        a_val = jnp.where(valid, a_val * a_sh, a_val)
    return a_val.reshape(LRU_CH, LRU_TILE), u_val.reshape(LRU_CH, LRU_TILE)


def _lru_carry(a_cum, h_loc, hc, rev):
    n_grp = LRU_CH // 8
    pieces = [None] * n_grp
    for j in (range(n_grp - 1, -1, -1) if rev else range(n_grp)):
        h = h_loc[8 * j:8 * j + 8] + a_cum[8 * j:8 * j + 8] * hc
        pieces[j] = h
        hc = h[0:1] if rev else h[7:8]
    return jnp.concatenate(pieces, axis=0), hc


def _lru_kernel(seq_len, n_rows, lx_ref, lg_ref, cw_ref, cb_ref, wg_ref, bg_ref, lam_ref, h0_ref,
                y_ref, hfin_ref, ac_scr, hl_scr):
    n_ch = n_rows // LRU_CH
    hfin_ref[...] = jnp.zeros_like(hfin_ref)
    cw = cw_ref[...]
    cb = cb_ref[...]

    def seq_edges(start):
        return (start % seq_len) == 0, ((start + LRU_CH) % seq_len) == 0

    def forward_chunk(ci, hc):
        start = pl.multiple_of(ci * LRU_CH, LRU_CH)
        at_seq_start, at_seq_end = seq_edges(start)
        hr = LRU_HALO
        prev = lx_ref[pl.ds(pl.multiple_of(jnp.maximum(start - hr, 0), hr), hr), :].astype(F32)
        nxt = lx_ref[pl.ds(pl.multiple_of(jnp.minimum(start + LRU_CH, n_rows - hr), hr), hr), :].astype(F32)
        prev = jnp.where(at_seq_start, 0.0, prev)
        nxt = jnp.where(at_seq_end, 0.0, nxt)
        cur = lx_ref[pl.ds(start, LRU_CH), :].astype(F32)
        ext = jnp.concatenate([prev, cur, nxt], axis=0)
        n_ext = LRU_CH + 2 * hr
        xm2 = pltpu.roll(ext, 2, axis=0)[hr:hr + LRU_CH]
        xm1 = pltpu.roll(ext, 1, axis=0)[hr:hr + LRU_CH]
        xp1 = pltpu.roll(ext, n_ext - 1, axis=0)[hr:hr + LRU_CH]
        xb = cb + xm2 * cw[0:1] + xm1 * cw[1:2] + cur * cw[2:3] + xp1 * cw[3:4]
        xb_bf = xb.astype(BF16)
        rows = pl.ds(start, LRU_CH)
        for d in range(2):
            pre = _dot(xb_bf, wg_ref[d]) + bg_ref[d]
            r_g = _sigmoid(pre[:, :LRU_TILE])
            i_g = _sigmoid(pre[:, LRU_TILE:])
            log_a = (LRU_C * r_g) * (-_softplus(-lam_ref[d]))
            a_val = jnp.exp(log_a)
            th = jnp.tanh(log_a)
            q = -2.0 * th / (1.0 - th)
            root = jnp.where(q > 0.0, q * lax.rsqrt(q), 0.0)
            a_cum, h_loc = _lru_group_scan(a_val, root * i_g * xb, d == 1)
            if d == 0:
                hc = jnp.where(at_seq_start, h0_ref[0], hc)
                h_all, hc = _lru_carry(a_cum, h_loc, hc, False)
                y_ref[rows, :] = h_all
            else:
                ac_scr[rows, :] = a_cum
                hl_scr[rows, :] = h_loc

        @pl.when(at_seq_end)
        def _():
            hfin_ref[0, pl.ds(start // seq_len, 1), :] = hc

        return hc

    def backward_chunk(ci, hc):
        start = pl.multiple_of((n_ch - 1 - ci) * LRU_CH, LRU_CH)
        at_seq_start, at_seq_end = seq_edges(start)
        rows = pl.ds(start, LRU_CH)
        hc = jnp.where(at_seq_end, h0_ref[1], hc)
        h_all, hc = _lru_carry(ac_scr[rows, :], hl_scr[rows, :], hc, True)
        y_ref[rows, :] = (y_ref[rows, :] + h_all) * _gelu_tanh(lg_ref[rows, :].astype(F32))

        @pl.when(at_seq_start)
        def _():
            hfin_ref[1, pl.ds(start // seq_len, 1), :] = hc

        return hc

    zero = jnp.zeros((1, LRU_TILE), F32)
    lax.fori_loop(0, n_ch, forward_chunk, zero)
    lax.fori_loop(0, n_ch, backward_chunk, zero)


def _lru_scan(p, n_seq, seq_len, h0, wts):
    t = p.shape[0]
    n_rows = max(seq_len, 8 * LRU_CH)
    nb = t // n_rows
    nt = D_RNN // LRU_TILE
    chain = h0 is not None
    if not chain:
        h0 = jnp.zeros((1, 2, 1, D_RNN), F32)
        h0_spec = pl.BlockSpec((None, 2, 1, LRU_TILE), lambda i, j: (0, 0, 0, j))
    else:
        h0 = h0.reshape(n_seq, 2, 1, D_RNN)
        h0_spec = pl.BlockSpec((None, 2, 1, LRU_TILE), lambda i, j: (i, 0, 0, j))
    lg0 = COL_LG // LRU_TILE
    y, hfin = pl.pallas_call(
        functools.partial(_lru_kernel, seq_len, n_rows),
        grid=(nb, nt),
        in_specs=[
            pl.BlockSpec((n_rows, LRU_TILE), lambda i, j: (i, j)),
            pl.BlockSpec((n_rows, LRU_TILE), lambda i, j: (i, lg0 + j)),
            pl.BlockSpec((CONV_W, LRU_TILE), lambda i, j: (0, j)),
            pl.BlockSpec((1, LRU_TILE), lambda i, j: (0, j)),
            pl.BlockSpec((2, None, LRU_TILE, 2 * LRU_TILE), lambda i, j: (0, j, 0, 0)),
            pl.BlockSpec((2, None, 1, 2 * LRU_TILE), lambda i, j: (0, j, 0, 0)),
            pl.BlockSpec((2, 1, LRU_TILE), lambda i, j: (0, 0, j)),
            h0_spec,
        ],
        out_specs=[
            pl.BlockSpec((n_rows, LRU_TILE), lambda i, j: (i, j)),
            pl.BlockSpec((2, None, 8, LRU_TILE), lambda i, j: (0, i, 0, j)),
        ],
        out_shape=[jax.ShapeDtypeStruct((t, D_RNN), F32), jax.ShapeDtypeStruct((2, nb, 8, D_RNN), F32)],
        scratch_shapes=[pltpu.VMEM((n_rows, LRU_TILE), F32), pltpu.VMEM((n_rows, LRU_TILE), F32)],
        compiler_params=_cparams(("parallel", "parallel")),
        name="lru_scan",
    )(p, p, wts["conv_w"], wts["conv_b"], wts["lru_wg"], wts["lru_bg"], wts["lru_lam"], h0)
    return y, hfin


def _lru_gate_weights(wa, ba, wx, bx):
    nt = D_RNN // LRU_TILE
    per = LRU_TILE // LRU_BLOCK
    eye = jnp.eye(per, dtype=F32)

    def tiles(w):
        w = w.reshape(2, nt, per, LRU_BLOCK, LRU_BLOCK)
        bd = jnp.einsum("dtaij,ab->dtaibj", w, eye)
        return bd.reshape(2, nt, LRU_TILE, LRU_TILE)

    wg = jnp.concatenate([tiles(wa), tiles(wx)], axis=-1).astype(BF16)
    bg = jnp.concatenate([ba.reshape(2, nt, 1, LRU_TILE), bx.reshape(2, nt, 1, LRU_TILE)], axis=-1)
    return wg, bg


ROUTER_LANES = 128
ROUTE_TILE = 256
PACK_W = 256


def _pack_bf16_pairs(x):
    bits = pltpu.bitcast(x.astype(BF16).astype(F32), jnp.uint32)
    hi_mask = jnp.uint32(0xFFFF0000)

    def pack(hi, lo):
        return (hi & hi_mask) | (lo >> 16)

    return (pack(bits[:, 0:PACK_W], bits[:, PACK_W:2 * PACK_W]),
            pack(bits[:, 2 * PACK_W:3 * PACK_W], bits[:, 3 * PACK_W:4 * PACK_W]))


def _unpack_bf16_pairs(pa, pb):
    hi_mask = jnp.uint32(0xFFFF0000)
    parts = []
    for p in (pa, pb):
        parts.append(pltpu.bitcast(p & hi_mask, F32))
        parts.append(pltpu.bitcast(p << 16, F32))
    return jnp.concatenate(parts, axis=1)


def _head_sum(x, ones_bd):
    m, n_g = x.shape[0], x.shape[1] // HG
    stacked = jnp.concatenate([x[:, g * HG:(g + 1) * HG] for g in range(n_g)], axis=0)
    sums = _dot_exact_rhs(stacked, ones_bd)
    return jnp.concatenate([sums[g * m:(g + 1) * m] for g in range(n_g)], axis=1)


def _merge_kernel(o0_ref, o1_ref, rkv_ref, sm_ref, ga_ref, gb_ref, yb_ref, x_ref, mod_ref,
                  aup_ref, a0_ref, ka_ref, rk_ref, gup_ref, lng_ref, lnb_ref, proj_ref, bm_ref,
                  wout_ref, g2_ref, rw_ref, rb_ref, x1_ref, ha_ref, hb_ref, route_ref, cnt_ref):
    ones_bd = _head_ones(HG)
    o = o0_ref[...] + o1_ref[...]
    mu = _head_sum(o, ones_bd) * (1.0 / HEAD_DIM)
    oc = o - mu
    var = _head_sum(oc * oc, ones_bd) * (1.0 / HEAD_DIM)
    on = oc * lax.rsqrt(var + GN_EPS)
    r = rkv_ref[0].astype(F32)
    k = rkv_ref[1].astype(F32)
    v = rkv_ref[2].astype(F32)
    sm = sm_ref[...]
    ad = sm[:, 64:128]
    gd = sm[:, 128:256]
    ka = ka_ref[...]
    mix = (rk_ref[0] + rk_ref[1]) * (1.0 - 0.5 * ka)
    for d in range(2):
        t_d = jnp.tanh(0.5 * (a0_ref[d] + _dot3_stacked(ad, aup_ref[d])))
        mix = mix + (0.5 * ka * rk_ref[d]) * t_d
    bonus = _head_sum(r * (k * mix), ones_bd) * v
    g = _dot(_sigmoid(gd).astype(BF16), gup_ref[...])
    y_a = (on * lng_ref[...] + lnb_ref[...] + bonus) * g
    y_b = _dot(yb_ref[...].astype(BF16), proj_ref[...])
    bm = bm_ref[...]
    gate_a = _sigmoid(ga_ref[...].astype(F32) + bm[:, :D_MODEL])
    gate_b = _sigmoid(gb_ref[...].astype(F32) + bm[:, D_MODEL:])
    y = gate_a * y_a + gate_b * y_b
    m = mod_ref[...]
    gt1 = m[:, 2 * D_MODEL:3 * D_MODEL]
    sh2 = m[:, 3 * D_MODEL:4 * D_MODEL]
    sc2 = m[:, 4 * D_MODEL:5 * D_MODEL]
    x1 = x_ref[...] + gt1 * _dot(y.astype(BF16), wout_ref[...])
    x1_ref[...] = x1
    ms = jnp.mean(x1 * x1, axis=-1, keepdims=True)
    h2 = (x1 * lax.rsqrt(ms + RMS_EPS) * g2_ref[...]) * (1.0 + sc2) + sh2
    ha_ref[...], hb_ref[...] = _pack_bf16_pairs(h2)

    hh, hl = _split2(h2)
    rwh = rw_ref[0]
    logits = _dot(hh, rwh) + (_dot(hl, rwh) + _dot(hh, rw_ref[1])) + rb_ref[...]
    lane = lax.broadcasted_iota(jnp.int32, logits.shape, 1)
    work = logits
    vals, sels, firsts = [], [], []
    for _ in range(TOP_K):
        mx = jnp.max(work, axis=-1, keepdims=True)
        first = jnp.min(jnp.where(work == mx, lane, ROUTER_LANES), axis=-1, keepdims=True)
        sel = lane == first
        vals.append(mx)
        sels.append(sel)
        firsts.append(first)
        work = jnp.where(sel, -jnp.inf, work)
    es = [jnp.exp(val - vals[0]) for val in vals]
    inv = 1.0 / (es[0] + es[1] + es[2] + es[3])
    mask = jnp.where(sels[0] | sels[1] | sels[2] | sels[3], 1.0, 0.0)
    tm = mask.shape[0]
    ri = lax.broadcasted_iota(jnp.int32, (tm, tm), 0)
    ci = lax.broadcasted_iota(jnp.int32, (tm, tm), 1)
    earlier = jnp.where(ci < ri, 1.0, 0.0).astype(BF16)
    rank = _dot(earlier, mask.astype(BF16))
    route = jnp.zeros_like(logits)
    for j in range(TOP_K):
        rank_j = jnp.sum(jnp.where(sels[j], rank, 0.0), axis=-1, keepdims=True)
        route = jnp.where(lane == j, firsts[j].astype(F32), route)
        route = jnp.where(lane == TOP_K + j, es[j] * inv, route)
        route = jnp.where(lane == 2 * TOP_K + j, rank_j, route)
    route_ref[...] = route
    cnt_ref[...] = jnp.broadcast_to(jnp.sum(mask, axis=0, keepdims=True), cnt_ref.shape)


def _merge(o0, o1, rkv, small, p, yb, x, mod3, mod_base, rows_per_mod, w):
    t = x.shape[0]
    tm = ROUTE_TILE

    def row(width, col=0):
        return pl.BlockSpec((tm, width), lambda i: (i, col))

    def full(shape):
        nd = len(shape)
        return pl.BlockSpec(shape, lambda i: (0,) * nd)

    return pl.pallas_call(
        _merge_kernel,
        grid=(t // tm,),
        in_specs=[
            row(D_MODEL), row(D_MODEL),
            pl.BlockSpec((3, tm, D_MODEL), lambda i: (0, i, 0)),
            pl.BlockSpec((None, tm, 256), lambda i: (0, i, 0)),
            row(D_MODEL, COL_GA // D_MODEL), row(D_MODEL, COL_GB // D_MODEL),
            row(D_RNN), row(D_MODEL),
            pl.BlockSpec((None, 1, 6 * D_MODEL), lambda i: (mod_base + (i * tm) // rows_per_mod, 0, 0)),
            full((2, 3 * 64, D_MODEL)), full((2, 1, D_MODEL)), full((1, D_MODEL)), full((2, 1, D_MODEL)),
            full((128, D_MODEL)), full((1, D_MODEL)), full((1, D_MODEL)), full((D_RNN, D_MODEL)),
            full((1, 2 * D_MODEL)), full((D_MODEL, D_MODEL)), full((1, D_MODEL)),
            full((2, D_MODEL, ROUTER_LANES)), full((1, ROUTER_LANES)),
        ],
        out_specs=[row(D_MODEL), row(PACK_W), row(PACK_W), row(ROUTER_LANES),
                   pl.BlockSpec((None, 8, ROUTER_LANES), lambda i: (i, 0, 0))],
        out_shape=[jax.ShapeDtypeStruct((t, D_MODEL), F32),
                   jax.ShapeDtypeStruct((t, PACK_W), jnp.uint32), jax.ShapeDtypeStruct((t, PACK_W), jnp.uint32),
                   jax.ShapeDtypeStruct((t, ROUTER_LANES), F32),
                   jax.ShapeDtypeStruct((t // tm, 8, ROUTER_LANES), F32)],
        compiler_params=_cparams(("parallel",)),
        name="merge",
    )(o0, o1, rkv, small, p, p, yb, x, mod3,
      w["a_up"], w["a0"], w["k_a"], w["r_k"], w["g_up"], w["ln_g"], w["ln_b"], w["lru_proj"],
      w["b_merge"], w["w_out"], w["norm_ffn_g"], w["router_w"], w["router_b"])


EXPERT_TILE = 1056
SC_WINDOW = 128


def _sc_mesh():
    return plsc.VectorSubcoreMesh(core_axis_name="core", subcore_axis_name="subcore")


def _sc_scatter_rows(x, pos_flat, n_out):
    t, width = x.shape
    n_idx = pos_flat.shape[1]
    n_src = t // SC_WINDOW

    @pl.kernel(out_type=jax.ShapeDtypeStruct((n_out, width), x.dtype), mesh=_sc_mesh(), scratch_types=[])
    def scatter(x_hbm, i_hbm, o_hbm):
        def body(x_vmem, i_vmem):
            pltpu.sync_copy(x_vmem, o_hbm.at[i_vmem.at[0]])

        pltpu.emit_pipeline(
            body,
            grid=(n_idx // SC_WINDOW,),
            in_specs=[pl.BlockSpec((SC_WINDOW, width), index_map=lambda i: (i % n_src, 0)),
                      pl.BlockSpec((1, SC_WINDOW), index_map=lambda i: (0, i))],
            out_specs=[],
            core_axis_name=("core", "subcore"),
            dimension_semantics=(pltpu.PARALLEL,),
        )(x_hbm, i_hbm)

    return scatter(x, pos_flat)


def _sc_gather_rows(y, pos_flat):
    n_idx = pos_flat.shape[1]
    width = y.shape[1]

    @pl.kernel(out_type=jax.ShapeDtypeStruct((n_idx, width), y.dtype), mesh=_sc_mesh(), scratch_types=[])
    def gather(y_hbm, i_hbm, o_hbm):
        def body(i_vmem, o_vmem):
            pltpu.sync_copy(y_hbm.at[i_vmem.at[0]], o_vmem)

        pltpu.emit_pipeline(
            body,
            grid=(n_idx // SC_WINDOW,),
            in_specs=[pl.BlockSpec((1, SC_WINDOW), index_map=lambda i: (0, i))],
            out_specs=[pl.BlockSpec((SC_WINDOW, width), index_map=lambda i: (i, 0))],
            core_axis_name=("core", "subcore"),
            dimension_semantics=(pltpu.PARALLEL,),
        )(i_hbm, o_hbm)

    return gather(y, pos_flat)


def _expert_kernel(te_ref, nt_ref, xa_ref, xb_ref, wgu_ref, bgu_ref, wd_ref, bd_ref, ya_ref, yb_ref,
                   wgu_bf, wd_bf):
    i = pl.program_id(0)
    prev = te_ref[jnp.maximum(i - 1, 0)]
    fresh = (i == 0) | (te_ref[i] != prev)

    @pl.when(fresh)
    def _():
        wgu_bf[...] = wgu_ref[...].astype(BF16)
        wd_bf[...] = wd_ref[...].astype(BF16)

    @pl.when(i < nt_ref[0])
    def _():
        d_exp = wd_bf.shape[0]
        bgu = bgu_ref[...]
        x = _unpack_bf16_pairs(xa_ref[...], xb_ref[...]).astype(BF16)
        gate = _dot(x, wgu_bf[:, :d_exp]) + bgu[:, :d_exp]
        up = _dot(x, wgu_bf[:, d_exp:]) + bgu[:, d_exp:]
        gate = jnp.minimum(gate, SWIGLU_LIMIT)
        up = jnp.clip(up, -SWIGLU_LIMIT, SWIGLU_LIMIT)
        act = gate * _sigmoid(SWIGLU_ALPHA * gate) * (up + 1.0)
        y = _dot(act.astype(BF16), wd_bf[...]) + bd_ref[...]
        ya_ref[...], yb_ref[...] = _pack_bf16_pairs(y)


def _experts(xs_a, xs_b, tile_expert, n_tiles_used, w):
    n_rows = xs_a.shape[0]
    n_tiles = n_rows // EXPERT_TILE
    n_exp, _, two_d = w["exp_w_gu"].shape
    d_exp = two_d // 2

    def rows(i, te, nt):
        return (jnp.minimum(i, nt[0] - 1), 0)

    def by_expert(i, te, nt):
        return (te[i], 0, 0)

    grid_spec = pltpu.PrefetchScalarGridSpec(
        num_scalar_prefetch=2,
        grid=(n_tiles,),
        in_specs=[
            pl.BlockSpec((EXPERT_TILE, PACK_W), rows),
            pl.BlockSpec((EXPERT_TILE, PACK_W), rows),
            pl.BlockSpec((None, D_MODEL, two_d), by_expert),
            pl.BlockSpec((None, 1, two_d), by_expert),
            pl.BlockSpec((None, d_exp, D_MODEL), by_expert),
            pl.BlockSpec((None, 1, D_MODEL), by_expert),
        ],
        out_specs=[pl.BlockSpec((EXPERT_TILE, PACK_W), rows), pl.BlockSpec((EXPERT_TILE, PACK_W), rows)],
        scratch_shapes=[pltpu.VMEM((D_MODEL, two_d), BF16), pltpu.VMEM((d_exp, D_MODEL), BF16)],
    )
    return pl.pallas_call(
        _expert_kernel,
        grid_spec=grid_spec,
        out_shape=[jax.ShapeDtypeStruct((n_rows, PACK_W), jnp.uint32)] * 2,
        compiler_params=_cparams(("arbitrary",)),
        name="experts",
    )(tile_expert, n_tiles_used, xs_a, xs_b, w["exp_w_gu"], w["exp_b_gu"], w["exp_w_down"], w["exp_b_down"])


def _combine_kernel(ya_ref, yb_ref, route_ref, x1_ref, mod_ref, fg_ref, o_ref):
    route = route_ref[...]
    lane = lax.broadcasted_iota(jnp.int32, route.shape, 1)
    acc = jnp.zeros(x1_ref.shape, F32)
    for j in range(TOP_K):
        w_j = jnp.sum(jnp.where(lane == TOP_K + j, route, 0.0), axis=-1, keepdims=True)
        acc = acc + w_j * _unpack_bf16_pairs(ya_ref[j], yb_ref[j])
    gt2 = mod_ref[...][:, 5 * D_MODEL:6 * D_MODEL]
    x2 = x1_ref[...] + gt2 * acc
    ms = jnp.mean(x2 * x2, axis=-1, keepdims=True)
    o_ref[...] = x2 * lax.rsqrt(ms + RMS_EPS) * fg_ref[...]


def _combine(yg_a, yg_b, route, x1, mod3, mod_base, rows_per_mod, w):
    t = x1.shape[0]
    tm = 1024
    return pl.pallas_call(
        _combine_kernel,
        grid=(t // tm,),
        in_specs=[
            pl.BlockSpec((TOP_K, tm, PACK_W), lambda i: (0, i, 0)),
            pl.BlockSpec((TOP_K, tm, PACK_W), lambda i: (0, i, 0)),
            pl.BlockSpec((tm, ROUTER_LANES), lambda i: (i, 0)),
            pl.BlockSpec((tm, D_MODEL), lambda i: (i, 0)),
            pl.BlockSpec((None, 1, 6 * D_MODEL), lambda i: (mod_base + (i * tm) // rows_per_mod, 0, 0)),
            pl.BlockSpec((1, D_MODEL), lambda i: (0, 0)),
        ],
        out_specs=pl.BlockSpec((tm, D_MODEL), lambda i: (i, 0)),
        out_shape=jax.ShapeDtypeStruct((t, D_MODEL), F32),
        compiler_params=_cparams(("parallel",)),
        name="combine",
    )(yg_a, yg_b, route, x1, mod3, w["final_norm_g"])


def _routing_tables(route, cnt, n_exp):
    t = route.shape[0]
    e4 = route[:, 0:TOP_K].astype(jnp.int32)
    r4 = route[:, 2 * TOP_K:3 * TOP_K].astype(jnp.int32)
    cnt = cnt[:, 0, :n_exp].astype(jnp.int32)
    tile_off = jnp.cumsum(cnt, axis=0) - cnt
    n_e = jnp.sum(cnt, axis=0)
    region = (n_e + EXPERT_TILE - 1) // EXPERT_TILE * EXPERT_TILE
    region_end = jnp.cumsum(region)
    base = region_end - region
    seg_start = jnp.repeat(base[None, :] + tile_off, ROUTE_TILE, axis=0)
    onehot = e4[:, :, None] == jnp.arange(n_exp, dtype=jnp.int32)[None, None, :]
    pos4 = jnp.sum(jnp.where(onehot, seg_start[:, None, :], 0), axis=-1) + r4
    n_tiles_max = (TOP_K * t + n_exp * (EXPERT_TILE - 1)) // EXPERT_TILE
    tile_start = jnp.arange(n_tiles_max, dtype=jnp.int32) * EXPERT_TILE
    regions_before = jnp.sum(tile_start[:, None] >= region_end[None, :], axis=1)
    tile_expert = jnp.minimum(regions_before, n_exp - 1)
    n_used = (region_end[-1] // EXPERT_TILE).reshape(1)
    return pos4.T, tile_expert.astype(jnp.int32), n_used.astype(jnp.int32), n_tiles_max


_N_SHIFT = 3328
_COL_RANGES = (
    (_N_SHIFT, _N_SHIFT + 2 * D_RNN),
    (0, 3 * D_MODEL),
    (_N_SHIFT + 2 * D_RNN, _N_SHIFT + 2 * D_RNN + 2 * D_MODEL),
    (3 * D_MODEL, _N_SHIFT),
)


def _permute_cols(a):
    return jnp.concatenate([a[..., lo:hi] for lo, hi in _COL_RANGES], axis=-1)


def _stream(x, n_seq, seq_len, mod3, mod_base, rows_per_mod, s_wkv0, s_lru0, grid_mode, w):
    t = n_seq * seq_len
    x2 = x.reshape(t, D_MODEL)
    p = _in_projection(x2, mod3, mod_base, rows_per_mod, w["norm_mix_g"], w["w_in"])
    rkv = _token_shift(p, w["mu"], COL_R, D_MODEL, (D_MODEL,), 3, seq_len, grid_mode, BF16)
    small = _token_shift(p, w["mu"], COL_SMALL, 256, (64, 64, 128), 1, seq_len, grid_mode, F32)
    o0, o1, s_wkv = _wkv_scan(rkv, small, n_seq, seq_len, s_wkv0, w)
    yb, s_lru = _lru_scan(p, n_seq, seq_len, s_lru0, w)
    x1, h_a, h_b, route, cnt = _merge(o0, o1, rkv, small, p, yb, x2, mod3, mod_base, rows_per_mod, w)
    return dict(x1=x1, h_a=h_a, h_b=h_b, route=route, cnt=cnt, s_wkv=s_wkv, s_lru=s_lru)


def _moe_and_norm(streams, mods, n_exp, w):
    h_a = jnp.concatenate([s["h_a"] for s in streams], axis=0)
    h_b = jnp.concatenate([s["h_b"] for s in streams], axis=0)
    route = jnp.concatenate([s["route"] for s in streams], axis=0)
    cnt = jnp.concatenate([s["cnt"] for s in streams], axis=0)
    pos, tile_expert, n_used, n_tiles_max = _routing_tables(route, cnt, n_exp)
    n_rows = n_tiles_max * EXPERT_TILE
    pos_flat = pos.reshape(1, -1)
    xs_a = _sc_scatter_rows(h_a, pos_flat, n_rows)
    xs_b = _sc_scatter_rows(h_b, pos_flat, n_rows)
    ys_a, ys_b = _experts(xs_a, xs_b, tile_expert, n_used, w)
    outs, row0 = [], 0
    for s, (mod3, mod_base, rows_per_mod) in zip(streams, mods):
        t = s["x1"].shape[0]
        pos_s = pos[:, row0:row0 + t].reshape(1, TOP_K * t)
        yg_a = _sc_gather_rows(ys_a, pos_s).reshape(TOP_K, t, PACK_W)
        yg_b = _sc_gather_rows(ys_b, pos_s).reshape(TOP_K, t, PACK_W)
        outs.append(_combine(yg_a, yg_b, s["route"], s["x1"], mod3, mod_base, rows_per_mod, w))
        row0 += t
    return outs


def kernel(x_prompt, x_sample, state_wkv, state_lru, c, c_ctx, norm_mix_g, norm_ffn_g, w_ada, b_ada, w_in, shift_mu, wkv_k_k, wkv_k_a, wkv_r_k, wkv_w0, wkv_w_up, wkv_a0, wkv_a_up, wkv_g_up, wkv_ln_g, wkv_ln_b, lru_conv_w, lru_conv_b, lru_wa, lru_ba, lru_wx, lru_bx, lru_lambda, lru_proj, b_merge, w_out, router_w, router_b, exp_w_gu, exp_b_gu, exp_w_down, exp_b_down, final_norm_g):
    n_ctx, seq, _ = x_prompt.shape
    n_dec, dec_seq, _ = x_sample.shape
    assert w_in.shape[0] == 1, "single trunk layer: the final norm is fused into the MoE kernel"
    assert 1 + n_dec <= 8 and dec_seq % GRID_W == 0
    l = 0
    n_exp = router_w.shape[-1]
    cond8 = jnp.concatenate([c_ctx[None, :], c, jnp.zeros((8 - 1 - n_dec, D_MODEL), F32)], axis=0)
    mod3 = _modulation(cond8, w_ada[l], b_ada[l]).reshape(8, 1, 6 * D_MODEL)
    mu_full = jnp.concatenate([shift_mu[l], jnp.zeros((N_IN - _N_SHIFT,), F32)])
    lru_wg, lru_bg = _lru_gate_weights(lru_wa[l], lru_ba[l], lru_wx[l], lru_bx[l])
    router_w_pad = jnp.zeros((D_MODEL, ROUTER_LANES), F32).at[:, :n_exp].set(router_w[l])
    router_b_pad = jnp.full((1, ROUTER_LANES), -1e30, F32).at[0, :n_exp].set(router_b[l])
    w = dict(
        norm_mix_g=norm_mix_g[l],
        w_in=_permute_cols(w_in[l]).astype(BF16),
        mu=_permute_cols(mu_full).reshape(1, N_IN),
        k_k=wkv_k_k[l].reshape(1, D_MODEL), k_a=wkv_k_a[l].reshape(1, D_MODEL),
        w0=wkv_w0[l].reshape(2, 1, D_MODEL), w_up=_stack3_rhs(wkv_w_up[l]),
        a0=wkv_a0[l].reshape(2, 1, D_MODEL), a_up=_stack3_rhs(wkv_a_up[l]),
        r_k=wkv_r_k[l].reshape(2, 1, D_MODEL), g_up=wkv_g_up[l].astype(BF16),
        ln_g=wkv_ln_g[l].reshape(1, D_MODEL), ln_b=wkv_ln_b[l].reshape(1, D_MODEL),
        conv_w=lru_conv_w[l], conv_b=lru_conv_b[l].reshape(1, D_RNN),
        lru_wg=lru_wg, lru_bg=lru_bg, lru_lam=lru_lambda[l].reshape(2, 1, D_RNN),
        lru_proj=lru_proj[l].astype(BF16), b_merge=b_merge[l].reshape(1, 2 * D_MODEL),
        w_out=w_out[l].astype(BF16), norm_ffn_g=norm_ffn_g[l].reshape(1, D_MODEL),
        router_w=jnp.stack(_split2(router_w_pad)), router_b=router_b_pad,
        exp_w_gu=exp_w_gu[l], exp_b_gu=exp_b_gu[l].reshape(n_exp, 1, -1),
        exp_w_down=exp_w_down[l], exp_b_down=exp_b_down[l].reshape(n_exp, 1, D_MODEL),
        final_norm_g=final_norm_g.reshape(1, D_MODEL),
    )
    ctx = _stream(x_prompt, n_ctx, seq, mod3, 0, n_ctx * seq, None, None, False, w)
    smp = _stream(x_sample, n_dec, dec_seq, mod3, 1, dec_seq, state_wkv[:, l], state_lru[:, l], True, w)
    y_prompt, y_sample = _moe_and_norm([ctx, smp], [(mod3, 0, n_ctx * seq), (mod3, 1, dec_seq)], n_exp, w)
    new_lru = ctx["s_lru"].reshape(2, n_ctx, D_RNN).transpose(1, 0, 2)
    return (y_prompt.reshape(x_prompt.shape), y_sample.reshape(x_sample.shape),
            ctx["s_wkv"][:, None].astype(state_wkv.dtype), new_lru[:, None].astype(state_lru.dtype))
```
